```python
import jax, jax.numpy as jnp
from jax import lax
import numpy as np

D_MODEL = 1024
BATCH = 4
SEQ = 4096
DEPTH = 1
DEC_BATCH = 128
DEC_SEQ = 8
PAST_LEN = 16384
PAGE_SIZE = 128

CONV_DIM = 512
CONV_WIDTH = 31
N_HEADS = 8
QK_NOPE = 64
QK_ROPE = 32
QK_HEAD = QK_NOPE + QK_ROPE
V_HEAD = 64
Q_LORA = 384
KV_LORA = 256
ROPE_BASE = 10000.0
SCALE = QK_HEAD ** -0.5
Q_BLOCK = 128
D_FF = 2816
EPS = 1e-6
IN_SPLITS = (2 * CONV_DIM,
             2 * CONV_DIM + Q_LORA,
             2 * CONV_DIM + Q_LORA + KV_LORA,
             2 * CONV_DIM + Q_LORA + KV_LORA + QK_ROPE,
             2 * CONV_DIM + Q_LORA + KV_LORA + QK_ROPE + D_MODEL)
IN_COLS = 2 * CONV_DIM + Q_LORA + KV_LORA + QK_ROPE + 2 * D_MODEL

kernel_name = 'macaron_conv_mla_gated_hybrid_step'


def rmsnorm(x, g):
    xf = x.astype(jnp.float32)
    y = xf * lax.rsqrt(jnp.mean(xf * xf, axis=-1, keepdims=True) + EPS)
    return (y * g.astype(jnp.float32)).astype(x.dtype)


def layernorm(x, g, b):
    xf = x.astype(jnp.float32)
    mu = jnp.mean(xf, axis=-1, keepdims=True)
    var = jnp.mean(jnp.square(xf - mu), axis=-1, keepdims=True)
    y = (xf - mu) * lax.rsqrt(var + EPS) * g.astype(jnp.float32) + b.astype(jnp.float32)
    return y.astype(x.dtype)


def swiglu_ffn(x, norm_g, w_gu, w_down):
    h = rmsnorm(x, norm_g)
    g, u = jnp.split(h @ w_gu, 2, axis=-1)
    return (jax.nn.silu(g) * u) @ w_down


def rope_cos_sin(pos):
    inv = ROPE_BASE ** (-jnp.arange(0, QK_ROPE, 2, dtype=jnp.float32) / QK_ROPE)
    ang = pos.astype(jnp.float32)[:, None] * inv[None, :]
    return jnp.cos(ang), jnp.sin(ang)


def apply_rope(x, cos, sin):
    x1, x2 = jnp.split(x.astype(jnp.float32), 2, axis=-1)
    out = jnp.concatenate([x1 * cos - x2 * sin, x2 * cos + x1 * sin], axis=-1)
    return out.astype(x.dtype)


def build_keys(c_kv, k_pe, w_uk, k_head_norm):
    k_nope = jnp.einsum('...lc,chd->...lhd', c_kv, w_uk)
    k_rope = jnp.broadcast_to(k_pe[..., None, :], k_nope.shape[:-1] + (QK_ROPE,))
    return rmsnorm(jnp.concatenate([k_nope, k_rope], axis=-1), k_head_norm)


def attn_core(q, k, c_kv, q_pos, k_pos, w_uv):
    s = jnp.einsum('qhd,khd->hqk', q, k, preferred_element_type=jnp.float32) * SCALE
    mask = k_pos[None, :] <= q_pos[:, None]
    s = jnp.where(mask[None], s, -jnp.inf)
    p = jax.nn.softmax(s, axis=-1).astype(c_kv.dtype)
    o_lat = jnp.einsum('hqk,kc->qhc', p, c_kv)
    return jnp.einsum('qhc,chd->qhd', o_lat, w_uv)


def conv_branch(u_ext, conv_w, conv_b, conv_ln_g, conv_ln_b, w_conv_out):
    y = lax.conv_general_dilated(u_ext, conv_w[:, None, :], window_strides=(1,), padding='VALID',
                                 dimension_numbers=('NWC', 'WIO', 'NWC'),
                                 feature_group_count=CONV_DIM) + conv_b
    y = jax.nn.silu(layernorm(y, conv_ln_g, conv_ln_b))
    return y @ w_conv_out


def prompt_attend(q, c_kv, k_pe, lw):
    b, t = q.shape[:2]
    k = build_keys(c_kv, k_pe, lw['w_uk'], lw['k_head_norm'])
    n_blk = t // Q_BLOCK
    qb = q.reshape(b, n_blk, Q_BLOCK, N_HEADS, QK_HEAD).swapaxes(0, 1)
    k_pos = jnp.arange(t)
    w_uv = lw['w_uv']

    def one_block(args):
        q_blk, blk = args
        q_pos = blk * Q_BLOCK + jnp.arange(Q_BLOCK)
        return jax.vmap(attn_core, in_axes=(0, 0, 0, None, None, None))(q_blk, k, c_kv, q_pos, k_pos, w_uv)

    o = lax.map(one_block, (qb, jnp.arange(n_blk)))
    return o.swapaxes(0, 1).reshape(b, t, N_HEADS * V_HEAD)


def make_sample_attend(cache_ckv_l, cache_kpe_l, page_table):
    def attend(q, c_kv, k_pe, lw):
        b, t = q.shape[:2]
        past = page_table.shape[1] * PAGE_SIZE
        q_pos = past + jnp.arange(t)
        k_pos = jnp.arange(past + t)
        w_uk, k_head_norm, w_uv = lw['w_uk'], lw['k_head_norm'], lw['w_uv']

        def one_seq(args):
            q_s, ckv_new, kpe_new, pages = args
            ckv = jnp.concatenate([cache_ckv_l[pages].reshape(past, KV_LORA).astype(ckv_new.dtype), ckv_new], axis=0)
            kpe = jnp.concatenate([cache_kpe_l[pages].reshape(past, QK_ROPE).astype(kpe_new.dtype), kpe_new], axis=0)
            k = build_keys(ckv, kpe, w_uk, k_head_norm)
            return attn_core(q_s, k, ckv, q_pos, k_pos, w_uv)

        o = lax.map(one_seq, (q, c_kv, k_pe, page_table))
        return o.reshape(b, t, N_HEADS * V_HEAD)
    return attend


def trunk_layer(x, pos, conv_prefix, attend, lw):
    b, t = x.shape[:2]
    x = x + 0.5 * swiglu_ffn(x, lw['norm_ffn1'], lw['ffn1_w_gu'], lw['ffn1_w_down'])
    h = rmsnorm(x, lw['norm_mix'])
    glu_in, q_a, c_kv, k_pe, gate_a, gate_b = jnp.split(h @ lw['w_in'], IN_SPLITS, axis=-1)
    ga, gb = jnp.split(glu_in, 2, axis=-1)
    u = ga * jax.nn.sigmoid(gb)
    u_ext = jnp.concatenate([conv_prefix.astype(u.dtype), u], axis=1)
    y_conv = conv_branch(u_ext, lw['conv_w'], lw['conv_b'], lw['conv_ln_g'], lw['conv_ln_b'], lw['w_conv_out'])
    cos, sin = rope_cos_sin(pos)
    q = (rmsnorm(q_a, lw['q_a_norm']) @ lw['w_uq']).reshape(b, t, N_HEADS, QK_HEAD)
    q = jnp.concatenate([q[..., :QK_NOPE], apply_rope(q[..., QK_NOPE:], cos[:, None, :], sin[:, None, :])], axis=-1)
    q = rmsnorm(q, lw['q_head_norm'])
    c_kv = rmsnorm(c_kv, lw['kv_a_norm'])
    k_pe = apply_rope(k_pe, cos, sin)
    y_attn = attend(q, c_kv, k_pe, lw) @ lw['w_attn_out']
    m = jax.nn.sigmoid(gate_a) * y_conv + jax.nn.sigmoid(gate_b) * y_attn
    x = x + m @ lw['w_out']
    x = x + 0.5 * swiglu_ffn(x, lw['norm_ffn2'], lw['ffn2_w_gu'], lw['ffn2_w_down'])
    return x, c_kv, k_pe, u_ext[:, -(CONV_WIDTH - 1):]


def setup_inputs(seed: int = 0) -> dict:
    key = jax.random.key(seed)
    ks = iter(jax.random.split(key, 40))
    n_pages = PAST_LEN // PAGE_SIZE
    n_pool = (DEC_BATCH * n_pages * 5) // 4

    def dense(shape, fan_in):
        return jax.random.normal(next(ks), (DEPTH,) + shape, jnp.float32) * fan_in ** -0.5

    def gain(n):
        return 1.0 + 0.05 * jax.random.normal(next(ks), (DEPTH, n), jnp.float32)

    def bias(n):
        return 0.02 * jax.random.normal(next(ks), (DEPTH, n), jnp.float32)

    x_prompt = jax.random.normal(next(ks), (BATCH, SEQ, D_MODEL), jnp.float32)
    x_sample = jax.random.normal(next(ks), (DEC_BATCH, DEC_SEQ, D_MODEL), jnp.float32)
    cache_ckv = jax.random.normal(next(ks), (DEPTH, n_pool, PAGE_SIZE, KV_LORA), jnp.float32)
    cache_kpe = jax.random.normal(next(ks), (DEPTH, n_pool, PAGE_SIZE, QK_ROPE), jnp.float32)
    state_conv = jax.random.normal(next(ks), (DEPTH, DEC_BATCH, CONV_WIDTH - 1, CONV_DIM), jnp.float32)
    page_table = jax.random.permutation(next(ks), n_pool)[:DEC_BATCH * n_pages].reshape(DEC_BATCH, n_pages).astype(jnp.int32)
    return {
        'x_prompt': x_prompt, 'x_sample': x_sample,
        'cache_ckv': cache_ckv, 'cache_kpe': cache_kpe, 'state_conv': state_conv,
        'page_table': page_table,
        'norm_ffn1': gain(D_MODEL),
        'ffn1_w_gu': dense((D_MODEL, 2 * D_FF), D_MODEL),
        'ffn1_w_down': dense((D_FF, D_MODEL), D_FF),
        'norm_mix': gain(D_MODEL),
        'w_in': dense((D_MODEL, IN_COLS), D_MODEL),
        'conv_w': dense((CONV_WIDTH, CONV_DIM), CONV_WIDTH),
        'conv_b': bias(CONV_DIM),
        'conv_ln_g': gain(CONV_DIM),
        'conv_ln_b': bias(CONV_DIM),
        'w_conv_out': dense((CONV_DIM, D_MODEL), CONV_DIM),
        'q_a_norm': gain(Q_LORA),
        'w_uq': dense((Q_LORA, N_HEADS * QK_HEAD), Q_LORA),
        'kv_a_norm': gain(KV_LORA),
        'w_uk': dense((KV_LORA, N_HEADS, QK_NOPE), KV_LORA),
        'w_uv': dense((KV_LORA, N_HEADS, V_HEAD), KV_LORA),
        'q_head_norm': gain(QK_HEAD),
        'k_head_norm': gain(QK_HEAD),
        'w_attn_out': dense((N_HEADS * V_HEAD, D_MODEL), N_HEADS * V_HEAD),
        'w_out': dense((D_MODEL, D_MODEL), D_MODEL),
        'norm_ffn2': gain(D_MODEL),
        'ffn2_w_gu': dense((D_MODEL, 2 * D_FF), D_MODEL),
        'ffn2_w_down': dense((D_FF, D_MODEL), D_FF),
    }


def reference(x_prompt, x_sample, cache_ckv, cache_kpe, state_conv, page_table,
              norm_ffn1, ffn1_w_gu, ffn1_w_down, norm_mix, w_in,
              conv_w, conv_b, conv_ln_g, conv_ln_b, w_conv_out,
              q_a_norm, w_uq, kv_a_norm, w_uk, w_uv, q_head_norm, k_head_norm, w_attn_out,
              w_out, norm_ffn2, ffn2_w_gu, ffn2_w_down):
    weights = {
        'norm_ffn1': norm_ffn1, 'ffn1_w_gu': ffn1_w_gu, 'ffn1_w_down': ffn1_w_down,
        'norm_mix': norm_mix, 'w_in': w_in,
        'conv_w': conv_w, 'conv_b': conv_b, 'conv_ln_g': conv_ln_g, 'conv_ln_b': conv_ln_b,
        'w_conv_out': w_conv_out,
        'q_a_norm': q_a_norm, 'w_uq': w_uq, 'kv_a_norm': kv_a_norm, 'w_uk': w_uk, 'w_uv': w_uv,
        'q_head_norm': q_head_norm, 'k_head_norm': k_head_norm, 'w_attn_out': w_attn_out,
        'w_out': w_out, 'norm_ffn2': norm_ffn2, 'ffn2_w_gu': ffn2_w_gu, 'ffn2_w_down': ffn2_w_down,
    }
    past = page_table.shape[1] * PAGE_SIZE
    pos_prompt = jnp.arange(x_prompt.shape[1])
    pos_sample = past + jnp.arange(x_sample.shape[1])
    conv_zero = jnp.zeros((x_prompt.shape[0], CONV_WIDTH - 1, CONV_DIM), x_prompt.dtype)
    yp, ys = x_prompt, x_sample
    ckv_p, kpe_p, conv_p, ckv_s, kpe_s, conv_s = [], [], [], [], [], []
    for l in range(DEPTH):
        lw = {name: w[l] for name, w in weights.items()}
        yp, c, kp, cs = trunk_layer(yp, pos_prompt, conv_zero, prompt_attend, lw)
        ckv_p.append(c); kpe_p.append(kp); conv_p.append(cs)
        ys, c, kp, cs = trunk_layer(ys, pos_sample, state_conv[l],
                                    make_sample_attend(cache_ckv[l], cache_kpe[l], page_table), lw)
        ckv_s.append(c); kpe_s.append(kp); conv_s.append(cs)
    return (yp, ys, jnp.stack(ckv_p), jnp.stack(kpe_p), jnp.stack(conv_p),
            jnp.stack(ckv_s), jnp.stack(kpe_s), jnp.stack(conv_s))
```

```python
import functools

import jax
import jax.numpy as jnp
from jax import lax
from jax.experimental import pallas as pl
from jax.experimental.pallas import tpu as pltpu

D_MODEL = 1024
CONV_DIM = 512
CONV_WIDTH = 31
N_HEADS = 8
QK_NOPE = 64
QK_ROPE = 32
QK_HEAD = QK_NOPE + QK_ROPE
V_HEAD = 64
Q_LORA = 384
KV_LORA = 256
ROPE_BASE = 10000.0
SCALE = QK_HEAD ** -0.5
D_FF = 2816
EPS = 1e-6
PAGE_SIZE = 128

HEAD_PAD = 128
FF_CHUNK = 256
CONV_HALO = 32
NEG_BIG = -1e30
VMEM_LIMIT = 56 * 1024 * 1024

BF16 = jnp.bfloat16
F32 = jnp.float32


def _row_tile(m, want=512):
    return want if m % want == 0 else m


def _rms(x, g):
    return x * lax.rsqrt(jnp.mean(x * x, axis=-1, keepdims=True) + EPS) * g


def _dot(a, b):
    return jnp.dot(a, b, preferred_element_type=F32)


def _dot_nt(a, b):
    return lax.dot_general(a, b, (((1,), (1,)), ((), ())), preferred_element_type=F32)


def _const_spec(shape):
    nd = len(shape)
    return pl.BlockSpec(shape, lambda *_: (0,) * nd)


def _params(sem):
    return pltpu.CompilerParams(dimension_semantics=sem, vmem_limit_bytes=VMEM_LIMIT)


def _ffn_kernel(x_ref, g_ref, wgu_ref, wd_ref, o_ref, acc_ref):
    x = x_ref[...]
    h = _rms(x, g_ref[...]).astype(BF16)
    for c in range(D_FF // FF_CHUNK):
        lo = c * FF_CHUNK
        gate = _dot(h, wgu_ref[:, lo:lo + FF_CHUNK])
        up = _dot(h, wgu_ref[:, D_FF + lo:D_FF + lo + FF_CHUNK])
        act = (gate * jax.nn.sigmoid(gate) * up).astype(BF16)
        part = _dot(act, wd_ref[lo:lo + FF_CHUNK, :])
        if c == 0:
            acc_ref[...] = part
        else:
            acc_ref[...] += part
    o_ref[...] = x + 0.5 * acc_ref[...]


def _ffn(x, g, wgu, wd):
    m = x.shape[0]
    tm = _row_tile(m)
    return pl.pallas_call(
        _ffn_kernel,
        out_shape=jax.ShapeDtypeStruct((m, D_MODEL), F32),
        grid=(m // tm,),
        in_specs=[
            pl.BlockSpec((tm, D_MODEL), lambda i: (i, 0)),
            _const_spec((1, D_MODEL)),
            _const_spec((D_MODEL, 2 * D_FF)),
            _const_spec((D_FF, D_MODEL)),
        ],
        out_specs=pl.BlockSpec((tm, D_MODEL), lambda i: (i, 0)),
        scratch_shapes=[pltpu.VMEM((tm, D_MODEL), F32)],
        compiler_params=_params(("parallel",)),
        name="ffn",
    )(x, g, wgu, wd)


_W1_QA = 2 * CONV_DIM
_W1_CKV = _W1_QA + Q_LORA
_W1_KPE = _W1_CKV + KV_LORA
_W1_KPE_SW = _W1_KPE + HEAD_PAD
_W1_COLS = _W1_KPE_SW + HEAD_PAD
_QW = N_HEADS * HEAD_PAD


def _inproj_kernel(x_ref, g_ref, w1_ref, gqa_ref, gkv_ref, wq_ref, cos_ref, sin_ref,
                   gq_ref, gk_ref, *rest, with_keys):
    if with_keys:
        wk_ref, wv_ref, u_ref, q_ref, c_ref, kpe_ref, k_ref, v_ref = rest
    else:
        u_ref, q_ref, c_ref, kpe_ref = rest
    h = _rms(x_ref[...], g_ref[...]).astype(BF16)
    proj = _dot(h, w1_ref[...])
    u_ref[...] = proj[:, :CONV_DIM] * jax.nn.sigmoid(proj[:, CONV_DIM:2 * CONV_DIM])

    cos = cos_ref[...]
    sin = sin_ref[...]
    qa = _rms(proj[:, _W1_QA:_W1_CKV], gqa_ref[...]).astype(BF16)
    qq = _dot(qa, wq_ref[...])
    gq = gq_ref[...]
    gk = gk_ref[...]
    for hd in range(N_HEADS):
        lo = hd * HEAD_PAD
        qh = qq[:, lo:lo + HEAD_PAD] * cos + qq[:, _QW + lo:_QW + lo + HEAD_PAD] * sin
        qh = qh * lax.rsqrt(jnp.sum(qh * qh, axis=-1, keepdims=True) * (1.0 / QK_HEAD) + EPS) * gq
        if not with_keys:
            qh = qh * gk
        q_ref[:, lo:lo + HEAD_PAD] = qh.astype(BF16)

    c = _rms(proj[:, _W1_CKV:_W1_KPE], gkv_ref[...])
    c_ref[...] = c
    kpe = proj[:, _W1_KPE:_W1_KPE_SW] * cos + proj[:, _W1_KPE_SW:_W1_COLS] * sin
    kpe_ref[...] = kpe

    if with_keys:
        cb = c.astype(BF16)
        kn = _dot(cb, wk_ref[...])
        for hd in range(N_HEADS):
            lo = hd * HEAD_PAD
            kh = kn[:, lo:lo + HEAD_PAD] + kpe
            kh = kh * lax.rsqrt(jnp.sum(kh * kh, axis=-1, keepdims=True) * (1.0 / QK_HEAD) + EPS) * gk
            k_ref[:, lo:lo + HEAD_PAD] = kh.astype(BF16)
        v_ref[...] = _dot(cb, wv_ref[...]).astype(BF16)


def _inproj(x, g, w1, gqa, gkv, wq, cos_tab, sin_tab, gq, gk, wk=None, wv=None):
    m = x.shape[0]
    tm = _row_tile(m)
    with_keys = wk is not None
    n_tab = cos_tab.shape[0] // tm
    row = lambda i: (i, 0)
    tab = lambda i: (i % n_tab, 0)
    in_specs = [
        pl.BlockSpec((tm, D_MODEL), row),
        _const_spec((1, D_MODEL)),
        _const_spec((D_MODEL, _W1_COLS)),
        _const_spec((1, Q_LORA)),
        _const_spec((1, KV_LORA)),
        _const_spec((Q_LORA, 2 * _QW)),
        pl.BlockSpec((tm, HEAD_PAD), tab),
        pl.BlockSpec((tm, HEAD_PAD), tab),
        _const_spec((1, HEAD_PAD)),
        _const_spec((1, HEAD_PAD)),
    ]
    args = [x, g, w1, gqa, gkv, wq, cos_tab, sin_tab, gq, gk]
    out_shape = [
        jax.ShapeDtypeStruct((m, CONV_DIM), F32),
        jax.ShapeDtypeStruct((m, _QW), BF16),
        jax.ShapeDtypeStruct((m, KV_LORA), F32),
        jax.ShapeDtypeStruct((m, HEAD_PAD), F32),
    ]
    out_specs = [
        pl.BlockSpec((tm, CONV_DIM), row),
        pl.BlockSpec((tm, _QW), row),
        pl.BlockSpec((tm, KV_LORA), row),
        pl.BlockSpec((tm, HEAD_PAD), row),
    ]
    if with_keys:
        in_specs += [_const_spec((KV_LORA, _QW)), _const_spec((KV_LORA, N_HEADS * V_HEAD))]
        args += [wk, wv]
        out_shape += [jax.ShapeDtypeStruct((m, _QW), BF16),
                      jax.ShapeDtypeStruct((m, N_HEADS * V_HEAD), BF16)]
        out_specs += [pl.BlockSpec((tm, _QW), row), pl.BlockSpec((tm, N_HEADS * V_HEAD), row)]
    return pl.pallas_call(
        functools.partial(_inproj_kernel, with_keys=with_keys),
        out_shape=out_shape,
        grid=(m // tm,),
        in_specs=in_specs,
        out_specs=out_specs,
        compiler_params=_params(("parallel",)),
        name="inproj_keys" if with_keys else "inproj",
    )(*args)


def _conv_tail(acc, b, g, beta):
    y = acc + b
    mu = jnp.mean(y, axis=-1, keepdims=True)
    d = y - mu
    var = jnp.mean(d * d, axis=-1, keepdims=True)
    z = d * lax.rsqrt(var + EPS) * g + beta
    return (z * jax.nn.sigmoid(z)).astype(BF16)


def _conv_prompt_kernel(cur_ref, prev_ref, w_ref, b_ref, g_ref, beta_ref, o_ref, ext_ref):
    tt = cur_ref.shape[1]
    first = pl.program_id(1) == 0
    prev = prev_ref[0]
    ext_ref[0:CONV_HALO, :] = jnp.where(first, 0.0, prev)
    ext_ref[CONV_HALO:, :] = cur_ref[0]
    base = CONV_HALO - (CONV_WIDTH - 1)
    acc = w_ref[0:1, :] * ext_ref[base:base + tt, :]
    for j in range(1, CONV_WIDTH):
        acc = acc + w_ref[j:j + 1, :] * ext_ref[base + j:base + j + tt, :]
    o_ref[0] = _conv_tail(acc, b_ref[...], g_ref[...], beta_ref[...])


def _conv_prompt(u, w, b, g, beta):
    nb, t, _ = u.shape
    tt = _row_tile(t, 256)
    per = tt // CONV_HALO
    return pl.pallas_call(
        _conv_prompt_kernel,
        out_shape=jax.ShapeDtypeStruct((nb, t, CONV_DIM), BF16),
        grid=(nb, t // tt),
        in_specs=[
            pl.BlockSpec((1, tt, CONV_DIM), lambda bi, i: (bi, i, 0)),
            pl.BlockSpec((1, CONV_HALO, CONV_DIM), lambda bi, i: (bi, jnp.maximum(i * per - 1, 0), 0)),
            _const_spec((CONV_WIDTH, CONV_DIM)),
            _const_spec((1, CONV_DIM)),
            _const_spec((1, CONV_DIM)),
            _const_spec((1, CONV_DIM)),
        ],
        out_specs=pl.BlockSpec((1, tt, CONV_DIM), lambda bi, i: (bi, i, 0)),
        scratch_shapes=[pltpu.VMEM((tt + CONV_HALO, CONV_DIM), F32)],
        compiler_params=_params(("parallel", "parallel")),
        name="conv_prompt",
    )(u, u, w, b, g, beta)


def _conv_sample_kernel(ext_ref, w_ref, b_ref, g_ref, beta_ref, o_ref):
    t = o_ref.shape[1]
    acc = w_ref[0:1, :] * ext_ref[:, 0:t, :]
    for j in range(1, CONV_WIDTH):
        acc = acc + w_ref[j:j + 1, :] * ext_ref[:, j:j + t, :]
    o_ref[...] = _conv_tail(acc, b_ref[...], g_ref[...], beta_ref[...])


def _conv_sample(ext, w, b, g, beta):
    nb, rows, _ = ext.shape
    t = rows - (CONV_WIDTH - 1)
    bs = 16 if nb % 16 == 0 else nb
    return pl.pallas_call(
        _conv_sample_kernel,
        out_shape=jax.ShapeDtypeStruct((nb, t, CONV_DIM), BF16),
        grid=(nb // bs,),
        in_specs=[
            pl.BlockSpec((bs, rows, CONV_DIM), lambda i: (i, 0, 0)),
            _const_spec((CONV_WIDTH, CONV_DIM)),
            _const_spec((1, CONV_DIM)),
            _const_spec((1, CONV_DIM)),
            _const_spec((1, CONV_DIM)),
        ],
        out_specs=pl.BlockSpec((bs, t, CONV_DIM), lambda i: (i, 0, 0)),
        compiler_params=_params(("parallel",)),
        name="conv_sample",
    )(ext, w, b, g, beta)


def _prompt_attn_kernel(q_ref, k_ref, v_ref, o_ref, m_ref, l_ref, acc_ref, even_ref, *, tk):
    tq = q_ref.shape[1]
    qi = pl.program_id(1)
    row = lax.broadcasted_iota(jnp.int32, (tq, tk), 0)
    col = lax.broadcasted_iota(jnp.int32, (tq, tk), 1)
    lane = lax.broadcasted_iota(jnp.int32, (tq, HEAD_PAD), 1)

    def step(hd, j, masked):
        lo = hd * HEAD_PAD
        start = pl.multiple_of(j * tk, tk)
        kh = k_ref[0, pl.ds(start, tk), lo:lo + HEAD_PAD]
        vlo = (hd // 2) * HEAD_PAD
        vh = v_ref[0, pl.ds(start, tk), vlo:vlo + HEAD_PAD]
        s = _dot_nt(q_ref[0, :, lo:lo + HEAD_PAD], kh)
        if masked:
            s = jnp.where(col <= row, s, NEG_BIG)
        m_old = m_ref[...]
        m_new = jnp.maximum(m_old, jnp.max(s, axis=-1, keepdims=True))
        alpha = jnp.exp(m_old - m_new)
        p = jnp.exp(s - m_new)
        l_ref[...] = alpha * l_ref[...] + jnp.sum(p, axis=-1, keepdims=True)
        acc_ref[...] = alpha * acc_ref[...] + _dot(p.astype(BF16), vh)
        m_ref[...] = m_new

    for hd in range(N_HEADS):
        m_ref[...] = jnp.full(m_ref.shape, NEG_BIG, F32)
        l_ref[...] = jnp.zeros(l_ref.shape, F32)
        acc_ref[...] = jnp.zeros(acc_ref.shape, F32)

        def body(j, carry, hd=hd):
            step(hd, j, False)
            return carry

        lax.fori_loop(0, qi, body, 0)
        step(hd, qi, True)
        res = acc_ref[...] / l_ref[...]
        if hd % 2 == 0:
            even_ref[...] = res
        else:
            vlo = (hd // 2) * HEAD_PAD
            o_ref[0, :, vlo:vlo + HEAD_PAD] = jnp.where(lane < V_HEAD, even_ref[...], res).astype(BF16)


def _prompt_attn(q, k, v):
    nb, t, _ = q.shape
    tq = _row_tile(t)
    return pl.pallas_call(
        functools.partial(_prompt_attn_kernel, tk=tq),
        out_shape=jax.ShapeDtypeStruct((nb, t, N_HEADS * V_HEAD), BF16),
        grid=(nb, t // tq),
        in_specs=[
            pl.BlockSpec((1, tq, _QW), lambda bi, i: (bi, i, 0)),
            pl.BlockSpec((1, t, _QW), lambda bi, i: (bi, 0, 0)),
            pl.BlockSpec((1, t, N_HEADS * V_HEAD), lambda bi, i: (bi, 0, 0)),
        ],
        out_specs=pl.BlockSpec((1, tq, N_HEADS * V_HEAD), lambda bi, i: (bi, i, 0)),
        scratch_shapes=[pltpu.VMEM((tq, 1), F32), pltpu.VMEM((tq, 1), F32),
                        pltpu.VMEM((tq, HEAD_PAD), F32), pltpu.VMEM((tq, HEAD_PAD), F32)],
        compiler_params=_params(("parallel", "arbitrary")),
        name="prompt_attn",
    )(q, k, v)


def _qabs_kernel(q_ref, w_ref, o_ref):
    o_ref[0] = _dot(q_ref[...], w_ref[0]).astype(BF16)


def _qabs(q, wukt):
    m = q.shape[0]
    return pl.pallas_call(
        _qabs_kernel,
        out_shape=jax.ShapeDtypeStruct((N_HEADS, m, KV_LORA), BF16),
        grid=(N_HEADS,),
        in_specs=[
            pl.BlockSpec((m, HEAD_PAD), lambda h: (0, h)),
            pl.BlockSpec((1, HEAD_PAD, KV_LORA), lambda h: (h, 0, 0)),
        ],
        out_specs=pl.BlockSpec((1, m, KV_LORA), lambda h: (h, 0, 0)),
        compiler_params=_params(("parallel",)),
        name="qabs",
    )(q, wukt)


def _sample_attn_kernel(pt_ref, qabs_ref, qr_ref, cnew_ref, kpenew_ref, wukt_ref, wuv_ref,
                        ckv_hbm, kpe_hbm, o_ref,
                        cbuf, kbuf, sem, m_ref, l_ref, acc_ref, cpad, kpad, wq_all,
                        *, layer, pages_per_chunk, sub):
    b = pl.program_id(0)
    nb = pl.num_programs(0)
    n_pages = pt_ref.shape[1]
    n_chunks = n_pages // pages_per_chunk
    chunk = pages_per_chunk * PAGE_SIZE
    t_new = cnew_ref.shape[1]
    rows = N_HEADS * t_new

    def copies(seq, ck, slot):
        out = []
        for p in range(pages_per_chunk):
            page = pt_ref[seq, ck * pages_per_chunk + p]
            dst = pl.ds(p * PAGE_SIZE, PAGE_SIZE)
            out.append(pltpu.make_async_copy(ckv_hbm.at[layer, page], cbuf.at[slot, dst], sem.at[0, slot]))
            out.append(pltpu.make_async_copy(kpe_hbm.at[layer, page], kbuf.at[slot, dst], sem.at[1, slot]))
        return out

    @pl.when(b == 0)
    def _():
        for cp in copies(0, 0, 0):
            cp.start()

    m_ref[...] = jnp.full(m_ref.shape, NEG_BIG, F32)
    l_ref[...] = jnp.zeros(l_ref.shape, F32)
    acc_ref[...] = jnp.zeros(acc_ref.shape, F32)

    n_k = N_HEADS * QK_NOPE
    wq_all[0:n_k, :] = wukt_ref[...]
    wq_all[n_k:, :] = qabs_ref[0]
    qr = qr_ref[0]
    ones = jnp.ones((8, QK_ROPE), BF16)

    def attend(c32, kp32, mask):
        n = c32.shape[0]
        cb = c32.astype(BF16)
        kt = _dot_nt(wq_all[...], cb)
        ss = jnp.sum((kt[:n_k] * kt[:n_k]).reshape(N_HEADS, QK_NOPE, n), axis=1)
        ss = ss + _dot_nt(ones, (kp32 * kp32).astype(BF16))
        r = lax.rsqrt(ss * (1.0 / QK_HEAD) + EPS)
        s = kt[n_k:] + _dot_nt(qr, kp32.astype(BF16))
        s = (s.reshape(N_HEADS, t_new, n) * r[:, None, :]).reshape(rows, n)
        if mask is not None:
            s = jnp.where(mask, s, NEG_BIG)
        m_old = m_ref[...]
        m_new = jnp.maximum(m_old, jnp.max(s, axis=-1, keepdims=True))
        alpha = jnp.exp(m_old - m_new)
        p = jnp.exp(s - m_new)
        l_ref[...] = alpha * l_ref[...] + jnp.sum(p, axis=-1, keepdims=True)
        acc_ref[...] = alpha * acc_ref[...] + _dot(p.astype(BF16), cb)
        m_ref[...] = m_new

    def chunk_body(ck, carry):
        g = b * n_chunks + ck
        slot = g % 2
        last = ck == n_chunks - 1
        nxt_seq = jnp.where(last, b + 1, b)
        nxt_ck = jnp.where(last, 0, ck + 1)

        @pl.when(jnp.logical_or(jnp.logical_not(last), b + 1 < nb))
        def _():
            for cp in copies(nxt_seq, nxt_ck, 1 - slot):
                cp.start()

        for cp in copies(b, ck, slot):
            cp.wait()
        for s0 in range(0, chunk, sub):
            attend(cbuf[slot, pl.ds(s0, sub), :], kbuf[slot, pl.ds(s0, sub), :], None)
        return carry

    lax.fori_loop(0, n_chunks, chunk_body, 0)

    cpad[...] = jnp.zeros(cpad.shape, F32)
    kpad[...] = jnp.zeros(kpad.shape, F32)
    cpad[0:t_new, :] = cnew_ref[0]
    kpad[0:t_new, :] = kpenew_ref[0]
    qpos = lax.broadcasted_iota(jnp.int32, (rows, PAGE_SIZE), 0) % t_new
    kpos = lax.broadcasted_iota(jnp.int32, (rows, PAGE_SIZE), 1)
    attend(cpad[...], kpad[...], kpos <= qpos)

    o_lat = (acc_ref[...] / l_ref[...]).astype(BF16)
    full = _dot(o_lat, wuv_ref[...])
    head_of_lane = lax.broadcasted_iota(jnp.int32, (t_new, N_HEADS * V_HEAD), 1) // V_HEAD
    out = full[0:t_new]
    for hd in range(1, N_HEADS):
        out = jnp.where(head_of_lane == hd, full[hd * t_new:(hd + 1) * t_new], out)
    o_ref[0] = out.astype(BF16)


def _sample_attn(page_table, qabs, qr, c_new, kpe_new, wukt, wuv, cache_ckv, cache_kpe, layer):
    nb, n_pages = page_table.shape
    t_new = c_new.shape[1]
    rows = N_HEADS * t_new
    ppc = 8 if n_pages % 16 == 0 else 1
    chunk = ppc * PAGE_SIZE
    sub = min(512, chunk)
    grid_spec = pltpu.PrefetchScalarGridSpec(
        num_scalar_prefetch=1,
        grid=(nb,),
        in_specs=[
            pl.BlockSpec((1, rows, KV_LORA), lambda b, pt: (b, 0, 0)),
            pl.BlockSpec((1, rows, QK_ROPE), lambda b, pt: (b, 0, 0)),
            pl.BlockSpec((1, t_new, KV_LORA), lambda b, pt: (b, 0, 0)),
            pl.BlockSpec((1, t_new, QK_ROPE), lambda b, pt: (b, 0, 0)),
            pl.BlockSpec((N_HEADS * QK_NOPE, KV_LORA), lambda b, pt: (0, 0)),
            pl.BlockSpec((KV_LORA, N_HEADS * V_HEAD), lambda b, pt: (0, 0)),
            pl.BlockSpec(memory_space=pl.ANY),
            pl.BlockSpec(memory_space=pl.ANY),
        ],
        out_specs=pl.BlockSpec((1, t_new, N_HEADS * V_HEAD), lambda b, pt: (b, 0, 0)),
        scratch_shapes=[
            pltpu.VMEM((2, chunk, KV_LORA), F32),
            pltpu.VMEM((2, chunk, QK_ROPE), F32),
            pltpu.SemaphoreType.DMA((2, 2)),
            pltpu.VMEM((rows, 1), F32),
            pltpu.VMEM((rows, 1), F32),
            pltpu.VMEM((rows, KV_LORA), F32),
            pltpu.VMEM((PAGE_SIZE, KV_LORA), F32),
            pltpu.VMEM((PAGE_SIZE, QK_ROPE), F32),
            pltpu.VMEM((N_HEADS * QK_NOPE + rows, KV_LORA), BF16),
        ],
    )
    return pl.pallas_call(
        functools.partial(_sample_attn_kernel, layer=layer, pages_per_chunk=ppc, sub=sub),
        out_shape=jax.ShapeDtypeStruct((nb, t_new, N_HEADS * V_HEAD), BF16),
        grid_spec=grid_spec,
        compiler_params=_params(("arbitrary",)),
        name="sample_attn",
    )(page_table, qabs, qr, c_new, kpe_new, wukt, wuv, cache_ckv, cache_kpe)


def _merge_kernel(x_ref, g_ref, a_ref, o_ref, wg_ref, wc_ref, wa_ref, wo_ref, y_ref):
    x = x_ref[...]
    h = _rms(x, g_ref[...]).astype(BF16)
    gates = jax.nn.sigmoid(_dot(h, wg_ref[...]))
    y_conv = _dot(a_ref[...], wc_ref[...])
    y_attn = _dot(o_ref[...], wa_ref[...])
    mix = gates[:, :D_MODEL] * y_conv + gates[:, D_MODEL:] * y_attn
    y_ref[...] = x + _dot(mix.astype(BF16), wo_ref[...])


def _merge(x, g, a, o, wg, wc, wa, wo):
    m = x.shape[0]
    tm = _row_tile(m)
    row = lambda i: (i, 0)
    return pl.pallas_call(
        _merge_kernel,
        out_shape=jax.ShapeDtypeStruct((m, D_MODEL), F32),
        grid=(m // tm,),
        in_specs=[
            pl.BlockSpec((tm, D_MODEL), row),
            _const_spec((1, D_MODEL)),
            pl.BlockSpec((tm, CONV_DIM), row),
            pl.BlockSpec((tm, N_HEADS * V_HEAD), row),
            _const_spec((D_MODEL, 2 * D_MODEL)),
            _const_spec((CONV_DIM, D_MODEL)),
            _const_spec((N_HEADS * V_HEAD, D_MODEL)),
            _const_spec((D_MODEL, D_MODEL)),
        ],
        out_specs=pl.BlockSpec((tm, D_MODEL), row),
        compiler_params=_params(("parallel",)),
        name="merge",
    )(x, g, a, o, wg, wc, wa, wo)


def _rotate_half_cols(w):
    half = QK_ROPE // 2
    return jnp.concatenate([-w[..., half:], w[..., :half]], axis=-1)


def _rope_tables(pos):
    inv = ROPE_BASE ** (-jnp.arange(0, QK_ROPE, 2, dtype=F32) / QK_ROPE)
    ang = pos.astype(F32)[:, None] * inv[None, :]
    cos, sin = jnp.cos(ang), jnp.sin(ang)
    n = pos.shape[0]
    pad = jnp.zeros((n, HEAD_PAD - QK_HEAD), F32)
    cos_tab = jnp.concatenate([jnp.ones((n, QK_NOPE), F32), cos, cos, pad], axis=-1)
    sin_tab = jnp.concatenate([jnp.zeros((n, QK_NOPE), F32), sin, sin, pad], axis=-1)
    return cos_tab, sin_tab


def kernel(x_prompt, x_sample, cache_ckv, cache_kpe, state_conv, page_table, norm_ffn1, ffn1_w_gu, ffn1_w_down, norm_mix, w_in, conv_w, conv_b, conv_ln_g, conv_ln_b, w_conv_out, q_a_norm, w_uq, kv_a_norm, w_uk, w_uv, q_head_norm, k_head_norm, w_attn_out, w_out, norm_ffn2, ffn2_w_gu, ffn2_w_down):
    depth = norm_ffn1.shape[0]
    nb, t, _ = x_prompt.shape
    ns, ts, _ = x_sample.shape
    past = page_table.shape[1] * PAGE_SIZE
    mp, ms = nb * t, ns * ts

    cos_p, sin_p = _rope_tables(jnp.arange(t))
    cos_s, sin_s = _rope_tables(past + jnp.arange(ts))
    tms = _row_tile(ms)
    cos_s = jnp.tile(cos_s, (tms // ts, 1))
    sin_s = jnp.tile(sin_s, (tms // ts, 1))

    yp = x_prompt.reshape(mp, D_MODEL)
    ys = x_sample.reshape(ms, D_MODEL)
    outs = [[] for _ in range(6)]
    for l in range(depth):
        wi = w_in[l]
        w_kpe = wi[:, _W1_KPE:_W1_KPE + QK_ROPE]
        zeros_nope = jnp.zeros((D_MODEL, QK_NOPE), F32)
        zeros_pad = jnp.zeros((D_MODEL, HEAD_PAD - QK_HEAD), F32)
        w1 = jnp.concatenate([wi[:, :_W1_KPE], zeros_nope, w_kpe, zeros_pad,
                              zeros_nope, _rotate_half_cols(w_kpe), zeros_pad], axis=-1).astype(BF16)
        w_gates = wi[:, _W1_KPE + QK_ROPE:].astype(BF16)
        wq3 = w_uq[l].reshape(Q_LORA, N_HEADS, QK_HEAD)
        q_nope, q_rope = wq3[..., :QK_NOPE], wq3[..., QK_NOPE:]
        zq_nope = jnp.zeros_like(q_nope)
        zq_pad = jnp.zeros((Q_LORA, N_HEADS, HEAD_PAD - QK_HEAD), F32)
        wq = jnp.concatenate([
            jnp.concatenate([q_nope, q_rope, zq_pad], axis=-1).reshape(Q_LORA, _QW),
            jnp.concatenate([zq_nope, _rotate_half_cols(q_rope), zq_pad], axis=-1).reshape(Q_LORA, _QW),
        ], axis=-1).astype(BF16)
        wk = jnp.pad(w_uk[l], ((0, 0), (0, 0), (0, HEAD_PAD - QK_NOPE))).reshape(KV_LORA, _QW).astype(BF16)
        wv = w_uv[l].reshape(KV_LORA, N_HEADS * V_HEAD).astype(BF16)
        wukt = w_uk[l].reshape(KV_LORA, N_HEADS * QK_NOPE).T.astype(BF16)
        wukt_pad = jnp.pad(jnp.transpose(w_uk[l], (1, 2, 0)),
                           ((0, 0), (0, HEAD_PAD - QK_NOPE), (0, 0))).astype(BF16)
        gpad = jnp.zeros((HEAD_PAD - QK_HEAD,), F32)
        gq = jnp.concatenate([q_head_norm[l] * SCALE, gpad])[None, :]
        gk = jnp.concatenate([k_head_norm[l], gpad])[None, :]
        row = lambda v: v[None, :]
        ffn1 = (row(norm_ffn1[l]), ffn1_w_gu[l].astype(BF16), ffn1_w_down[l].astype(BF16))
        ffn2 = (row(norm_ffn2[l]), ffn2_w_gu[l].astype(BF16), ffn2_w_down[l].astype(BF16))
        proj_w = (row(norm_mix[l]), w1, row(q_a_norm[l]), row(kv_a_norm[l]), wq)
        conv_p = (conv_w[l], row(conv_b[l]), row(conv_ln_g[l]), row(conv_ln_b[l]))
        merge_w = (w_gates, w_conv_out[l].astype(BF16), w_attn_out[l].astype(BF16), w_out[l].astype(BF16))

        x1 = _ffn(yp, *ffn1)
        u, q, c, kpe, k, v = _inproj(x1, *proj_w, cos_p, sin_p, gq, gk, wk, wv)
        u3 = u.reshape(nb, t, CONV_DIM)
        a = _conv_prompt(u3, *conv_p).reshape(mp, CONV_DIM)
        o = _prompt_attn(q.reshape(nb, t, _QW), k.reshape(nb, t, _QW),
                         v.reshape(nb, t, N_HEADS * V_HEAD)).reshape(mp, N_HEADS * V_HEAD)
        x2 = _merge(x1, row(norm_mix[l]), a, o, *merge_w)
        yp = _ffn(x2, *ffn2)
        outs[0].append(c.reshape(nb, t, KV_LORA))
        outs[1].append(kpe[:, QK_NOPE:QK_HEAD].reshape(nb, t, QK_ROPE))
        outs[2].append(u3[:, t - (CONV_WIDTH - 1):, :])

        x1 = _ffn(ys, *ffn1)
        u, q, c, kpe = _inproj(x1, *proj_w, cos_s, sin_s, gq, gk)
        ext = jnp.concatenate([state_conv[l], u.reshape(ns, ts, CONV_DIM)], axis=1)
        a = _conv_sample(ext, *conv_p).reshape(ms, CONV_DIM)
        c3 = c.reshape(ns, ts, KV_LORA)
        kpe3 = kpe[:, QK_NOPE:QK_HEAD].reshape(ns, ts, QK_ROPE)
        qabs = _qabs(q, wukt_pad)
        qabs = qabs.reshape(N_HEADS, ns, ts, KV_LORA).transpose(1, 0, 2, 3).reshape(ns, N_HEADS * ts, KV_LORA)
        qr = q.reshape(ns, ts, N_HEADS, HEAD_PAD)[..., QK_NOPE:QK_HEAD]
        qr = qr.transpose(0, 2, 1, 3).reshape(ns, N_HEADS * ts, QK_ROPE)
        o = _sample_attn(page_table, qabs, qr, c3, kpe3, wukt, wv, cache_ckv, cache_kpe, l)
        x2 = _merge(x1, row(norm_mix[l]), a, o.reshape(ms, N_HEADS * V_HEAD), *merge_w)
        ys = _ffn(x2, *ffn2)
        outs[3].append(c3)
        outs[4].append(kpe3)
        outs[5].append(ext[:, ts:, :])

    return (yp.reshape(nb, t, D_MODEL), ys.reshape(ns, ts, D_MODEL),
            jnp.stack(outs[0]), jnp.stack(outs[1]), jnp.stack(outs[2]),
            jnp.stack(outs[3]), jnp.stack(outs[4]), jnp.stack(outs[5]))
```

```python
import functools

import jax
import jax.numpy as jnp
from jax import lax
from jax.experimental import pallas as pl
from jax.experimental.pallas import tpu as pltpu

D_MODEL = 1024
CONV_DIM = 512
CONV_WIDTH = 31
N_HEADS = 8
QK_NOPE = 64
QK_ROPE = 32
QK_HEAD = QK_NOPE + QK_ROPE
V_HEAD = 64
Q_LORA = 384
KV_LORA = 256
ROPE_BASE = 10000.0
SCALE = QK_HEAD ** -0.5
LOG2E = 1.4426950408889634
D_FF = 2816
EPS = 1e-6
PAGE_SIZE = 128

SUBLANES = 8
HEAD_PAD = 128
FF_CHUNK = 256
CONV_HALO = 32
SAMPLE_PAGES_PER_CHUNK = 8
NEG_BIG = -1e30
VMEM_LIMIT = 56 * 1024 * 1024

BF16 = jnp.bfloat16
F32 = jnp.float32


def _row_tile(m, want=512):
    return want if m % want == 0 else m


def _rms(x, g):
    return x * lax.rsqrt(jnp.mean(x * x, axis=-1, keepdims=True) + EPS) * g


def _dot(a, b):
    return jnp.dot(a, b, preferred_element_type=F32)


def _dot_nt(a, b):
    return lax.dot_general(a, b, (((1,), (1,)), ((), ())), preferred_element_type=F32)


def _const_spec(shape):
    nd = len(shape)
    return pl.BlockSpec(shape, lambda *_: (0,) * nd)


def _params(sem):
    return pltpu.CompilerParams(dimension_semantics=sem, vmem_limit_bytes=VMEM_LIMIT)


def _ffn_kernel(x_ref, g_ref, wgu_ref, wd_ref, o_ref, acc_ref):
    x = x_ref[...]
    h = _rms(x, g_ref[...]).astype(BF16)
    for c in range(D_FF // FF_CHUNK):
        lo = c * FF_CHUNK
        gate = _dot(h, wgu_ref[:, lo:lo + FF_CHUNK])
        up = _dot(h, wgu_ref[:, D_FF + lo:D_FF + lo + FF_CHUNK])
        act = (gate * jax.nn.sigmoid(gate) * up).astype(BF16)
        part = _dot(act, wd_ref[lo:lo + FF_CHUNK, :])
        if c == 0:
            acc_ref[...] = part
        else:
            acc_ref[...] += part
    o_ref[...] = x + 0.5 * acc_ref[...]


def _ffn(x, g, wgu, wd):
    m = x.shape[0]
    tm = _row_tile(m)
    return pl.pallas_call(
        _ffn_kernel,
        out_shape=jax.ShapeDtypeStruct((m, D_MODEL), F32),
        grid=(m // tm,),
        in_specs=[
            pl.BlockSpec((tm, D_MODEL), lambda i: (i, 0)),
            _const_spec((1, D_MODEL)),
            _const_spec((D_MODEL, 2 * D_FF)),
            _const_spec((D_FF, D_MODEL)),
        ],
        out_specs=pl.BlockSpec((tm, D_MODEL), lambda i: (i, 0)),
        scratch_shapes=[pltpu.VMEM((tm, D_MODEL), F32)],
        compiler_params=_params(("parallel",)),
        name="ffn",
    )(x, g, wgu, wd)


_W1_QA = 2 * CONV_DIM
_W1_CKV = _W1_QA + Q_LORA
_W1_KPE = _W1_CKV + KV_LORA
_W1_KPE_SW = _W1_KPE + HEAD_PAD
_W1_COLS = _W1_KPE_SW + HEAD_PAD
_QW = N_HEADS * HEAD_PAD


def _inproj_kernel(x_ref, g_ref, w1_ref, gqa_ref, gkv_ref, wq_ref, cos_ref, sin_ref,
                   gq_ref, gk_ref, *rest, with_keys):
    if with_keys:
        wk_ref, wv_ref, u_ref, q_ref, c_ref, kpe_ref, k_ref, v_ref = rest
    else:
        u_ref, q_ref, c_ref, kpe_ref = rest
    h = _rms(x_ref[...], g_ref[...]).astype(BF16)
    proj = _dot(h, w1_ref[...])
    u_ref[...] = proj[:, :CONV_DIM] * jax.nn.sigmoid(proj[:, CONV_DIM:2 * CONV_DIM])

    cos = cos_ref[...]
    sin = sin_ref[...]
    qa = _rms(proj[:, _W1_QA:_W1_CKV], gqa_ref[...]).astype(BF16)
    qq = _dot(qa, wq_ref[...])
    gq = gq_ref[...]
    gk = gk_ref[...]
    for hd in range(N_HEADS):
        lo = hd * HEAD_PAD
        qh = qq[:, lo:lo + HEAD_PAD] * cos + qq[:, _QW + lo:_QW + lo + HEAD_PAD] * sin
        qh = qh * lax.rsqrt(jnp.sum(qh * qh, axis=-1, keepdims=True) * (1.0 / QK_HEAD) + EPS) * gq
        if not with_keys:
            qh = qh * gk
        q_ref[:, lo:lo + HEAD_PAD] = qh.astype(BF16)

    c = _rms(proj[:, _W1_CKV:_W1_KPE], gkv_ref[...])
    c_ref[...] = c
    kpe = proj[:, _W1_KPE:_W1_KPE_SW] * cos + proj[:, _W1_KPE_SW:_W1_COLS] * sin
    kpe_ref[...] = kpe

    if with_keys:
        cb = c.astype(BF16)
        kn = _dot(cb, wk_ref[...])
        for hd in range(N_HEADS):
            lo = hd * HEAD_PAD
            kh = kn[:, lo:lo + HEAD_PAD] + kpe
            kh = kh * lax.rsqrt(jnp.sum(kh * kh, axis=-1, keepdims=True) * (1.0 / QK_HEAD) + EPS) * gk
            k_ref[:, lo:lo + HEAD_PAD] = kh.astype(BF16)
        v_ref[...] = _dot(cb, wv_ref[...]).astype(BF16)


def _inproj(x, g, w1, gqa, gkv, wq, cos_tab, sin_tab, gq, gk, wk=None, wv=None):
    m = x.shape[0]
    tm = _row_tile(m)
    with_keys = wk is not None
    n_tab = cos_tab.shape[0] // tm
    row = lambda i: (i, 0)
    tab = lambda i: (i % n_tab, 0)
    in_specs = [
        pl.BlockSpec((tm, D_MODEL), row),
        _const_spec((1, D_MODEL)),
        _const_spec((D_MODEL, _W1_COLS)),
        _const_spec((1, Q_LORA)),
        _const_spec((1, KV_LORA)),
        _const_spec((Q_LORA, 2 * _QW)),
        pl.BlockSpec((tm, HEAD_PAD), tab),
        pl.BlockSpec((tm, HEAD_PAD), tab),
        _const_spec((1, HEAD_PAD)),
        _const_spec((1, HEAD_PAD)),
    ]
    args = [x, g, w1, gqa, gkv, wq, cos_tab, sin_tab, gq, gk]
    out_shape = [
        jax.ShapeDtypeStruct((m, CONV_DIM), F32),
        jax.ShapeDtypeStruct((m, _QW), BF16),
        jax.ShapeDtypeStruct((m, KV_LORA), F32),
        jax.ShapeDtypeStruct((m, HEAD_PAD), F32),
    ]
    out_specs = [
        pl.BlockSpec((tm, CONV_DIM), row),
        pl.BlockSpec((tm, _QW), row),
        pl.BlockSpec((tm, KV_LORA), row),
        pl.BlockSpec((tm, HEAD_PAD), row),
    ]
    if with_keys:
        in_specs += [_const_spec((KV_LORA, _QW)), _const_spec((KV_LORA, N_HEADS * V_HEAD))]
        args += [wk, wv]
        out_shape += [jax.ShapeDtypeStruct((m, _QW), BF16),
                      jax.ShapeDtypeStruct((m, N_HEADS * V_HEAD), BF16)]
        out_specs += [pl.BlockSpec((tm, _QW), row), pl.BlockSpec((tm, N_HEADS * V_HEAD), row)]
    return pl.pallas_call(
        functools.partial(_inproj_kernel, with_keys=with_keys),
        out_shape=out_shape,
        grid=(m // tm,),
        in_specs=in_specs,
        out_specs=out_specs,
        compiler_params=_params(("parallel",)),
        name="inproj_keys" if with_keys else "inproj",
    )(*args)


def _conv_tail(acc, b, g, beta):
    y = acc + b
    mu = jnp.mean(y, axis=-1, keepdims=True)
    d = y - mu
    var = jnp.mean(d * d, axis=-1, keepdims=True)
    z = d * lax.rsqrt(var + EPS) * g + beta
    return (z * jax.nn.sigmoid(z)).astype(BF16)


def _conv_prompt_kernel(cur_ref, prev_ref, w_ref, b_ref, g_ref, beta_ref, o_ref, ext_ref, shift_ref):
    tt = cur_ref.shape[1]
    first = pl.program_id(1) == 0
    prev = prev_ref[0]
    ext_ref[0:CONV_HALO, :] = jnp.where(first, 0.0, prev)
    ext_ref[CONV_HALO:, :] = cur_ref[0]
    span = shift_ref.shape[1]
    for s in range(1, SUBLANES):
        shift_ref[s - 1] = ext_ref[s:s + span, :]
    base = CONV_HALO - (CONV_WIDTH - 1)
    acc = None
    for j in range(CONV_WIDTH):
        s = (base + j) % SUBLANES
        lo = base + j - s
        rows = ext_ref[lo:lo + tt, :] if s == 0 else shift_ref[s - 1, lo:lo + tt, :]
        term = w_ref[j:j + 1, :] * rows
        acc = term if acc is None else acc + term
    o_ref[0] = _conv_tail(acc, b_ref[...], g_ref[...], beta_ref[...])


def _conv_prompt(u, w, b, g, beta):
    nb, t, _ = u.shape
    tt = _row_tile(t, 256)
    per = tt // CONV_HALO
    return pl.pallas_call(
        _conv_prompt_kernel,
        out_shape=jax.ShapeDtypeStruct((nb, t, CONV_DIM), BF16),
        grid=(nb, t // tt),
        in_specs=[
            pl.BlockSpec((1, tt, CONV_DIM), lambda bi, i: (bi, i, 0)),
            pl.BlockSpec((1, CONV_HALO, CONV_DIM), lambda bi, i: (bi, jnp.maximum(i * per - 1, 0), 0)),
            _const_spec((CONV_WIDTH, CONV_DIM)),
            _const_spec((1, CONV_DIM)),
            _const_spec((1, CONV_DIM)),
            _const_spec((1, CONV_DIM)),
        ],
        out_specs=pl.BlockSpec((1, tt, CONV_DIM), lambda bi, i: (bi, i, 0)),
        scratch_shapes=[pltpu.VMEM((tt + CONV_HALO, CONV_DIM), F32),
                        pltpu.VMEM((SUBLANES - 1, tt + CONV_HALO - SUBLANES, CONV_DIM), F32)],
        compiler_params=_params(("parallel", "parallel")),
        name="conv_prompt",
    )(u, u, w, b, g, beta)


def _conv_sample_kernel(ext_ref, w_ref, b_ref, g_ref, beta_ref, o_ref):
    t = o_ref.shape[1]
    acc = w_ref[0:1, :] * ext_ref[:, 0:t, :]
    for j in range(1, CONV_WIDTH):
        acc = acc + w_ref[j:j + 1, :] * ext_ref[:, j:j + t, :]
    o_ref[...] = _conv_tail(acc, b_ref[...], g_ref[...], beta_ref[...])


def _conv_sample(ext, w, b, g, beta):
    nb, rows, _ = ext.shape
    t = rows - (CONV_WIDTH - 1)
    bs = 16 if nb % 16 == 0 else nb
    return pl.pallas_call(
        _conv_sample_kernel,
        out_shape=jax.ShapeDtypeStruct((nb, t, CONV_DIM), BF16),
        grid=(nb // bs,),
        in_specs=[
            pl.BlockSpec((bs, rows, CONV_DIM), lambda i: (i, 0, 0)),
            _const_spec((CONV_WIDTH, CONV_DIM)),
            _const_spec((1, CONV_DIM)),
            _const_spec((1, CONV_DIM)),
            _const_spec((1, CONV_DIM)),
        ],
        out_specs=pl.BlockSpec((bs, t, CONV_DIM), lambda i: (i, 0, 0)),
        compiler_params=_params(("parallel",)),
        name="conv_sample",
    )(ext, w, b, g, beta)


def _prompt_attn_kernel(q_ref, k_ref, v_ref, o_ref, m_ref, l_ref, acc_ref, *, tk):
    tq = q_ref.shape[1]
    qi = pl.program_id(1)
    row = lax.broadcasted_iota(jnp.int32, (tq, tk), 0)
    col = lax.broadcasted_iota(jnp.int32, (tq, tk), 1)
    lane = lax.broadcasted_iota(jnp.int32, (tq, HEAD_PAD), 1)

    def step(hd, j, masked):
        lo = hd * HEAD_PAD
        start = pl.multiple_of(j * tk, tk)
        kh = k_ref[0, pl.ds(start, tk), lo:lo + HEAD_PAD]
        vlo = (hd // 2) * HEAD_PAD
        vh = v_ref[0, pl.ds(start, tk), vlo:vlo + HEAD_PAD]
        s = _dot_nt(q_ref[0, :, lo:lo + HEAD_PAD], kh)
        if masked:
            s = jnp.where(col <= row, s, NEG_BIG)
        m_old = m_ref[hd]
        m_new = jnp.maximum(m_old, jnp.max(s, axis=-1, keepdims=True))
        alpha = jnp.exp2(m_old - m_new)
        p = jnp.exp2(s - m_new)
        l_ref[hd] = alpha * l_ref[hd] + jnp.sum(p, axis=-1, keepdims=True)
        acc_ref[hd] = alpha * acc_ref[hd] + _dot(p.astype(BF16), vh)
        m_ref[hd] = m_new

    m_ref[...] = jnp.full(m_ref.shape, NEG_BIG, F32)
    l_ref[...] = jnp.zeros(l_ref.shape, F32)
    acc_ref[...] = jnp.zeros(acc_ref.shape, F32)

    def body(j, carry):
        for hd in range(N_HEADS):
            step(hd, j, False)
        return carry

    lax.fori_loop(0, qi, body, 0)
    for hd in range(N_HEADS):
        step(hd, qi, True)
    for pair in range(N_HEADS // 2):
        even = acc_ref[2 * pair] / l_ref[2 * pair]
        odd = acc_ref[2 * pair + 1] / l_ref[2 * pair + 1]
        vlo = pair * HEAD_PAD
        o_ref[0, :, vlo:vlo + HEAD_PAD] = jnp.where(lane < V_HEAD, even, odd).astype(BF16)


def _prompt_attn(q, k, v):
    nb, t, _ = q.shape
    tq = _row_tile(t)
    return pl.pallas_call(
        functools.partial(_prompt_attn_kernel, tk=tq),
        out_shape=jax.ShapeDtypeStruct((nb, t, N_HEADS * V_HEAD), BF16),
        grid=(nb, t // tq),
        in_specs=[
            pl.BlockSpec((1, tq, _QW), lambda bi, i: (bi, i, 0)),
            pl.BlockSpec((1, t, _QW), lambda bi, i: (bi, 0, 0)),
            pl.BlockSpec((1, t, N_HEADS * V_HEAD), lambda bi, i: (bi, 0, 0)),
        ],
        out_specs=pl.BlockSpec((1, tq, N_HEADS * V_HEAD), lambda bi, i: (bi, i, 0)),
        scratch_shapes=[pltpu.VMEM((N_HEADS, tq, 1), F32), pltpu.VMEM((N_HEADS, tq, 1), F32),
                        pltpu.VMEM((N_HEADS, tq, HEAD_PAD), F32)],
        compiler_params=_params(("parallel", "arbitrary")),
        name="prompt_attn",
    )(q, k, v)


def _qabs_kernel(q_ref, w_ref, o_ref):
    o_ref[0] = _dot(q_ref[...], w_ref[0]).astype(BF16)


def _qabs(q, wukt):
    m = q.shape[0]
    return pl.pallas_call(
        _qabs_kernel,
        out_shape=jax.ShapeDtypeStruct((N_HEADS, m, KV_LORA), BF16),
        grid=(N_HEADS,),
        in_specs=[
            pl.BlockSpec((m, HEAD_PAD), lambda h: (0, h)),
            pl.BlockSpec((1, HEAD_PAD, KV_LORA), lambda h: (h, 0, 0)),
        ],
        out_specs=pl.BlockSpec((1, m, KV_LORA), lambda h: (h, 0, 0)),
        compiler_params=_params(("parallel",)),
        name="qabs",
    )(q, wukt)


def _sample_attn_kernel(pt_ref, qabs_ref, qr_ref, cnew_ref, kpenew_ref, wukt_ref, wuv_ref,
                        ckv_hbm, kpe_hbm, o_ref,
                        cbuf, kbuf, sem, m_ref, l_ref, acc_ref, cpad, wq_all,
                        *, layer, pages_per_chunk):
    b = pl.program_id(0)
    nb = pl.num_programs(0)
    n_pages = pt_ref.shape[1]
    n_chunks = n_pages // pages_per_chunk
    chunk = pages_per_chunk * PAGE_SIZE
    t_new = cnew_ref.shape[1]
    rows = N_HEADS * t_new

    def copies(seq, ck, slot):
        out = []
        for p in range(pages_per_chunk):
            page = pt_ref[seq, ck * pages_per_chunk + p]
            dst = pl.ds(p * PAGE_SIZE, PAGE_SIZE)
            out.append(pltpu.make_async_copy(ckv_hbm.at[layer, page], cbuf.at[slot, dst], sem.at[0, slot]))
            out.append(pltpu.make_async_copy(kpe_hbm.at[layer, page], kbuf.at[slot, :, dst], sem.at[1, slot]))
        return out

    @pl.when(b == 0)
    def _():
        for cp in copies(0, 0, 0):
            cp.start()

    m_ref[...] = jnp.full(m_ref.shape, NEG_BIG, F32)
    l_ref[...] = jnp.zeros(l_ref.shape, F32)
    acc_ref[...] = jnp.zeros(acc_ref.shape, F32)

    n_k = N_HEADS * QK_NOPE
    wq_all[0:n_k, :] = wukt_ref[...]
    wq_all[n_k:, :] = qabs_ref[0]
    qr = qr_ref[0]

    def attend(c32, kpt32, mask):
        n = c32.shape[0]
        cb = c32.astype(BF16)
        kt = _dot_nt(wq_all[...], cb)
        ss = jnp.sum((kt[:n_k] * kt[:n_k]).reshape(QK_NOPE, N_HEADS, n), axis=0)
        ss = ss + jnp.sum(kpt32 * kpt32, axis=0, keepdims=True)
        r = lax.rsqrt(ss * (1.0 / QK_HEAD) + EPS)
        s = kt[n_k:] + _dot(qr, kpt32.astype(BF16))
        s = (s.reshape(t_new, N_HEADS, n) * r[None, :, :]).reshape(rows, n)
        if mask is not None:
            s = jnp.where(mask, s, NEG_BIG)
        m_old = m_ref[...]
        m_new = jnp.maximum(m_old, jnp.max(s, axis=-1, keepdims=True))
        alpha = jnp.exp2(m_old - m_new)
        p = jnp.exp2(s - m_new)
        l_ref[...] = alpha * l_ref[...] + jnp.sum(p, axis=-1, keepdims=True)
        acc_ref[...] = alpha * acc_ref[...] + _dot(p.astype(BF16), cb)
        m_ref[...] = m_new

    def chunk_body(ck, carry):
        g = b * n_chunks + ck
        slot = g % 2
        last = ck == n_chunks - 1
        nxt_seq = jnp.where(last, b + 1, b)
        nxt_ck = jnp.where(last, 0, ck + 1)

        @pl.when(jnp.logical_or(jnp.logical_not(last), b + 1 < nb))
        def _():
            for cp in copies(nxt_seq, nxt_ck, 1 - slot):
                cp.start()

        for cp in copies(b, ck, slot):
            cp.wait()
        attend(cbuf[slot], kbuf[slot], None)
        return carry

    lax.fori_loop(0, n_chunks, chunk_body, 0)

    cpad[...] = jnp.zeros(cpad.shape, F32)
    cpad[0:t_new, :] = cnew_ref[0]
    qpos = lax.broadcasted_iota(jnp.int32, (rows, PAGE_SIZE), 0) // N_HEADS
    kpos = lax.broadcasted_iota(jnp.int32, (rows, PAGE_SIZE), 1)
    attend(cpad[...], kpenew_ref[0], kpos <= qpos)

    o_lat = (acc_ref[...] / l_ref[...]).astype(BF16)
    full = _dot(o_lat, wuv_ref[...])
    full = full.reshape(t_new, N_HEADS, N_HEADS * V_HEAD)
    own = (lax.broadcasted_iota(jnp.int32, (N_HEADS, N_HEADS * V_HEAD), 1) // V_HEAD
           == lax.broadcasted_iota(jnp.int32, (N_HEADS, N_HEADS * V_HEAD), 0))
    o_ref[0] = jnp.sum(jnp.where(own[None], full, 0.0), axis=1).astype(BF16)


def _sample_attn(page_table, qabs, qr, c_new, kpe_new, wukt, wuv, cache_ckv, cache_kpe, layer):
    nb, n_pages = page_table.shape
    t_new = c_new.shape[1]
    rows = N_HEADS * t_new
    ppc = SAMPLE_PAGES_PER_CHUNK if n_pages % SAMPLE_PAGES_PER_CHUNK == 0 else 1
    chunk = ppc * PAGE_SIZE
    grid_spec = pltpu.PrefetchScalarGridSpec(
        num_scalar_prefetch=1,
        grid=(nb,),
        in_specs=[
            pl.BlockSpec((1, rows, KV_LORA), lambda b, pt: (b, 0, 0)),
            pl.BlockSpec((1, rows, QK_ROPE), lambda b, pt: (b, 0, 0)),
            pl.BlockSpec((1, t_new, KV_LORA), lambda b, pt: (b, 0, 0)),
            pl.BlockSpec((1, QK_ROPE, PAGE_SIZE), lambda b, pt: (b, 0, 0)),
            pl.BlockSpec((N_HEADS * QK_NOPE, KV_LORA), lambda b, pt: (0, 0)),
            pl.BlockSpec((KV_LORA, N_HEADS * V_HEAD), lambda b, pt: (0, 0)),
            pl.BlockSpec(memory_space=pl.ANY),
            pl.BlockSpec(memory_space=pl.ANY),
        ],
        out_specs=pl.BlockSpec((1, t_new, N_HEADS * V_HEAD), lambda b, pt: (b, 0, 0)),
        scratch_shapes=[
            pltpu.VMEM((2, chunk, KV_LORA), F32),
            pltpu.VMEM((2, QK_ROPE, chunk), F32),
            pltpu.SemaphoreType.DMA((2, 2)),
            pltpu.VMEM((rows, 1), F32),
            pltpu.VMEM((rows, 1), F32),
            pltpu.VMEM((rows, KV_LORA), F32),
            pltpu.VMEM((PAGE_SIZE, KV_LORA), F32),
            pltpu.VMEM((N_HEADS * QK_NOPE + rows, KV_LORA), BF16),
        ],
    )
    return pl.pallas_call(
        functools.partial(_sample_attn_kernel, layer=layer, pages_per_chunk=ppc),
        out_shape=jax.ShapeDtypeStruct((nb, t_new, N_HEADS * V_HEAD), BF16),
        grid_spec=grid_spec,
        compiler_params=_params(("arbitrary",)),
        name="sample_attn",
    )(page_table, qabs, qr, c_new, kpe_new, wukt, wuv, cache_ckv, cache_kpe)


def _merge_kernel(x_ref, g_ref, a_ref, o_ref, wg_ref, wc_ref, wa_ref, wo_ref, y_ref):
    x = x_ref[...]
    h = _rms(x, g_ref[...]).astype(BF16)
    gates = jax.nn.sigmoid(_dot(h, wg_ref[...]))
    y_conv = _dot(a_ref[...], wc_ref[...])
    y_attn = _dot(o_ref[...], wa_ref[...])
    mix = gates[:, :D_MODEL] * y_conv + gates[:, D_MODEL:] * y_attn
    y_ref[...] = x + _dot(mix.astype(BF16), wo_ref[...])


def _merge(x, g, a, o, wg, wc, wa, wo):
    m = x.shape[0]
    tm = _row_tile(m)
    row = lambda i: (i, 0)
    return pl.pallas_call(
        _merge_kernel,
        out_shape=jax.ShapeDtypeStruct((m, D_MODEL), F32),
        grid=(m // tm,),
        in_specs=[
            pl.BlockSpec((tm, D_MODEL), row),
            _const_spec((1, D_MODEL)),
            pl.BlockSpec((tm, CONV_DIM), row),
            pl.BlockSpec((tm, N_HEADS * V_HEAD), row),
            _const_spec((D_MODEL, 2 * D_MODEL)),
            _const_spec((CONV_DIM, D_MODEL)),
            _const_spec((N_HEADS * V_HEAD, D_MODEL)),
            _const_spec((D_MODEL, D_MODEL)),
        ],
        out_specs=pl.BlockSpec((tm, D_MODEL), row),
        compiler_params=_params(("parallel",)),
        name="merge",
    )(x, g, a, o, wg, wc, wa, wo)


def _rotate_half_cols(w):
    half = QK_ROPE // 2
    return jnp.concatenate([-w[..., half:], w[..., :half]], axis=-1)


def _rope_tables(pos):
    inv = ROPE_BASE ** (-jnp.arange(0, QK_ROPE, 2, dtype=F32) / QK_ROPE)
    ang = pos.astype(F32)[:, None] * inv[None, :]
    cos, sin = jnp.cos(ang), jnp.sin(ang)
    n = pos.shape[0]
    pad = jnp.zeros((n, HEAD_PAD - QK_HEAD), F32)
    cos_tab = jnp.concatenate([jnp.ones((n, QK_NOPE), F32), cos, cos, pad], axis=-1)
    sin_tab = jnp.concatenate([jnp.zeros((n, QK_NOPE), F32), sin, sin, pad], axis=-1)
    return cos_tab, sin_tab


def kernel(x_prompt, x_sample, cache_ckv, cache_kpe, state_conv, page_table, norm_ffn1, ffn1_w_gu, ffn1_w_down, norm_mix, w_in, conv_w, conv_b, conv_ln_g, conv_ln_b, w_conv_out, q_a_norm, w_uq, kv_a_norm, w_uk, w_uv, q_head_norm, k_head_norm, w_attn_out, w_out, norm_ffn2, ffn2_w_gu, ffn2_w_down):
    depth = norm_ffn1.shape[0]
    nb, t, _ = x_prompt.shape
    ns, ts, _ = x_sample.shape
    past = page_table.shape[1] * PAGE_SIZE
    mp, ms = nb * t, ns * ts

    cos_p, sin_p = _rope_tables(jnp.arange(t))
    cos_s, sin_s = _rope_tables(past + jnp.arange(ts))
    tms = _row_tile(ms)
    cos_s = jnp.tile(cos_s, (tms // ts, 1))
    sin_s = jnp.tile(sin_s, (tms // ts, 1))

    cache_kpe_t = jnp.swapaxes(cache_kpe, 2, 3)
    yp = x_prompt.reshape(mp, D_MODEL)
    ys = x_sample.reshape(ms, D_MODEL)
    outs = [[] for _ in range(6)]
    for l in range(depth):
        wi = w_in[l]
        w_kpe = wi[:, _W1_KPE:_W1_KPE + QK_ROPE]
        zeros_nope = jnp.zeros((D_MODEL, QK_NOPE), F32)
        zeros_pad = jnp.zeros((D_MODEL, HEAD_PAD - QK_HEAD), F32)
        w1 = jnp.concatenate([wi[:, :_W1_KPE], zeros_nope, w_kpe, zeros_pad,
                              zeros_nope, _rotate_half_cols(w_kpe), zeros_pad], axis=-1).astype(BF16)
        w_gates = wi[:, _W1_KPE + QK_ROPE:].astype(BF16)
        wq3 = w_uq[l].reshape(Q_LORA, N_HEADS, QK_HEAD)
        q_nope, q_rope = wq3[..., :QK_NOPE], wq3[..., QK_NOPE:]
        zq_nope = jnp.zeros_like(q_nope)
        zq_pad = jnp.zeros((Q_LORA, N_HEADS, HEAD_PAD - QK_HEAD), F32)
        wq = jnp.concatenate([
            jnp.concatenate([q_nope, q_rope, zq_pad], axis=-1).reshape(Q_LORA, _QW),
            jnp.concatenate([zq_nope, _rotate_half_cols(q_rope), zq_pad], axis=-1).reshape(Q_LORA, _QW),
        ], axis=-1).astype(BF16)
        wk = jnp.pad(w_uk[l], ((0, 0), (0, 0), (0, HEAD_PAD - QK_NOPE))).reshape(KV_LORA, _QW).astype(BF16)
        wv = w_uv[l].reshape(KV_LORA, N_HEADS * V_HEAD).astype(BF16)
        wukt = jnp.transpose(w_uk[l], (2, 1, 0)).reshape(N_HEADS * QK_NOPE, KV_LORA).astype(BF16)
        wukt_pad = jnp.pad(jnp.transpose(w_uk[l], (1, 2, 0)),
                           ((0, 0), (0, HEAD_PAD - QK_NOPE), (0, 0))).astype(BF16)
        gpad = jnp.zeros((HEAD_PAD - QK_HEAD,), F32)
        gq = jnp.concatenate([q_head_norm[l] * (SCALE * LOG2E), gpad])[None, :]
        gk = jnp.concatenate([k_head_norm[l], gpad])[None, :]
        row = lambda v: v[None, :]
        ffn1 = (row(norm_ffn1[l]), ffn1_w_gu[l].astype(BF16), ffn1_w_down[l].astype(BF16))
        ffn2 = (row(norm_ffn2[l]), ffn2_w_gu[l].astype(BF16), ffn2_w_down[l].astype(BF16))
        proj_w = (row(norm_mix[l]), w1, row(q_a_norm[l]), row(kv_a_norm[l]), wq)
        conv_p = (conv_w[l], row(conv_b[l]), row(conv_ln_g[l]), row(conv_ln_b[l]))
        merge_w = (w_gates, w_conv_out[l].astype(BF16), w_attn_out[l].astype(BF16), w_out[l].astype(BF16))

        x1 = _ffn(yp, *ffn1)
        u, q, c, kpe, k, v = _inproj(x1, *proj_w, cos_p, sin_p, gq, gk, wk, wv)
        u3 = u.reshape(nb, t, CONV_DIM)
        a = _conv_prompt(u3, *conv_p).reshape(mp, CONV_DIM)
        o = _prompt_attn(q.reshape(nb, t, _QW), k.reshape(nb, t, _QW),
                         v.reshape(nb, t, N_HEADS * V_HEAD)).reshape(mp, N_HEADS * V_HEAD)
        x2 = _merge(x1, row(norm_mix[l]), a, o, *merge_w)
        yp = _ffn(x2, *ffn2)
        outs[0].append(c.reshape(nb, t, KV_LORA))
        outs[1].append(kpe[:, QK_NOPE:QK_HEAD].reshape(nb, t, QK_ROPE))
        outs[2].append(u3[:, t - (CONV_WIDTH - 1):, :])

        x1 = _ffn(ys, *ffn1)
        u, q, c, kpe = _inproj(x1, *proj_w, cos_s, sin_s, gq, gk)
        ext = jnp.concatenate([state_conv[l], u.reshape(ns, ts, CONV_DIM)], axis=1)
        a = _conv_sample(ext, *conv_p).reshape(ms, CONV_DIM)
        c3 = c.reshape(ns, ts, KV_LORA)
        kpe3 = kpe[:, QK_NOPE:QK_HEAD].reshape(ns, ts, QK_ROPE)
        qabs = _qabs(q, wukt_pad)
        qabs = qabs.reshape(N_HEADS, ns, ts, KV_LORA).transpose(1, 2, 0, 3).reshape(ns, ts * N_HEADS, KV_LORA)
        qr = q.reshape(ns, ts, N_HEADS, HEAD_PAD)[..., QK_NOPE:QK_HEAD].reshape(ns, ts * N_HEADS, QK_ROPE)
        kpe_new_t = jnp.pad(kpe3.transpose(0, 2, 1), ((0, 0), (0, 0), (0, PAGE_SIZE - ts)))
        o = _sample_attn(page_table, qabs, qr, c3, kpe_new_t, wukt, wv, cache_ckv, cache_kpe_t, l)
        x2 = _merge(x1, row(norm_mix[l]), a, o.reshape(ms, N_HEADS * V_HEAD), *merge_w)
        ys = _ffn(x2, *ffn2)
        outs[3].append(c3)
        outs[4].append(kpe3)
        outs[5].append(ext[:, ts:, :])

    return (yp.reshape(nb, t, D_MODEL), ys.reshape(ns, ts, D_MODEL),
            jnp.stack(outs[0]), jnp.stack(outs[1]), jnp.stack(outs[2]),
            jnp.stack(outs[3]), jnp.stack(outs[4]), jnp.stack(outs[5]))
```

```python
import functools

import jax
import jax.numpy as jnp
from jax import lax
from jax.experimental import pallas as pl
from jax.experimental.pallas import tpu as pltpu

D_MODEL = 1024
CONV_DIM = 512
CONV_WIDTH = 31
N_HEADS = 8
QK_NOPE = 64
QK_ROPE = 32
QK_HEAD = QK_NOPE + QK_ROPE
V_HEAD = 64
Q_LORA = 384
KV_LORA = 256
ROPE_BASE = 10000.0
SCALE = QK_HEAD ** -0.5
LOG2E = 1.4426950408889634
D_FF = 2816
EPS = 1e-6
PAGE_SIZE = 128

SUBLANES = 8
HEAD_PAD = 128
FF_CHUNK = 256
CONV_HALO = 32
SAMPLE_PAGES_PER_CHUNK = 8
NEG_BIG = -1e30
SHIFT_SLACK = 1.01
MAX_SCORE_SHIFT = 50.0
VMEM_LIMIT = 56 * 1024 * 1024

BF16 = jnp.bfloat16
F32 = jnp.float32


def _row_tile(m, want=512):
    return want if m % want == 0 else m


def _rms(x, g):
    return x * lax.rsqrt(jnp.mean(x * x, axis=-1, keepdims=True) + EPS) * g


def _dot(a, b):
    return jnp.dot(a, b, preferred_element_type=F32)


def _dot_nt(a, b):
    return lax.dot_general(a, b, (((1,), (1,)), ((), ())), preferred_element_type=F32)


def _const_spec(shape):
    nd = len(shape)
    return pl.BlockSpec(shape, lambda *_: (0,) * nd)


def _params(sem):
    return pltpu.CompilerParams(dimension_semantics=sem, vmem_limit_bytes=VMEM_LIMIT)


def _ffn_kernel(x_ref, g_ref, wgu_ref, wd_ref, o_ref, acc_ref):
    x = x_ref[...]
    h = _rms(x, g_ref[...]).astype(BF16)
    for c in range(D_FF // FF_CHUNK):
        lo = c * FF_CHUNK
        gate = _dot(h, wgu_ref[:, lo:lo + FF_CHUNK])
        up = _dot(h, wgu_ref[:, D_FF + lo:D_FF + lo + FF_CHUNK])
        act = (gate * jax.nn.sigmoid(gate) * up).astype(BF16)
        part = _dot(act, wd_ref[lo:lo + FF_CHUNK, :])
        if c == 0:
            acc_ref[...] = part
        else:
            acc_ref[...] += part
    o_ref[...] = x + 0.5 * acc_ref[...]


def _ffn(x, g, wgu, wd):
    m = x.shape[0]
    tm = _row_tile(m)
    return pl.pallas_call(
        _ffn_kernel,
        out_shape=jax.ShapeDtypeStruct((m, D_MODEL), F32),
        grid=(m // tm,),
        in_specs=[
            pl.BlockSpec((tm, D_MODEL), lambda i: (i, 0)),
            _const_spec((1, D_MODEL)),
            _const_spec((D_MODEL, 2 * D_FF)),
            _const_spec((D_FF, D_MODEL)),
        ],
        out_specs=pl.BlockSpec((tm, D_MODEL), lambda i: (i, 0)),
        scratch_shapes=[pltpu.VMEM((tm, D_MODEL), F32)],
        compiler_params=_params(("parallel",)),
        name="ffn",
    )(x, g, wgu, wd)


_W1_QA = 2 * CONV_DIM
_W1_CKV = _W1_QA + Q_LORA
_W1_KPE = _W1_CKV + KV_LORA
_W1_KPE_SW = _W1_KPE + HEAD_PAD
_W1_COLS = _W1_KPE_SW + HEAD_PAD
_QW = N_HEADS * HEAD_PAD


def _inproj_kernel(x_ref, g_ref, w1_ref, gqa_ref, gkv_ref, wq_ref, cos_ref, sin_ref,
                   gq_ref, gk_ref, *rest, with_keys):
    if with_keys:
        wk_ref, wv_ref, aug_ref, u_ref, q_ref, c_ref, kpe_ref, k_ref, v_ref = rest
    else:
        u_ref, q_ref, c_ref, kpe_ref = rest
    h = _rms(x_ref[...], g_ref[...]).astype(BF16)
    proj = _dot(h, w1_ref[...])
    u_ref[...] = proj[:, :CONV_DIM] * jax.nn.sigmoid(proj[:, CONV_DIM:2 * CONV_DIM])

    cos = cos_ref[...]
    sin = sin_ref[...]
    qa = _rms(proj[:, _W1_QA:_W1_CKV], gqa_ref[...]).astype(BF16)
    qq = _dot(qa, wq_ref[...])
    gq = gq_ref[...]
    gk = gk_ref[...]
    for hd in range(N_HEADS):
        lo = hd * HEAD_PAD
        qh = qq[:, lo:lo + HEAD_PAD] * cos + qq[:, _QW + lo:_QW + lo + HEAD_PAD] * sin
        qh = qh * lax.rsqrt(jnp.sum(qh * qh, axis=-1, keepdims=True) * (1.0 / QK_HEAD) + EPS) * gq
        if with_keys:
            qb = qh.astype(BF16).astype(F32)
            qn = jnp.sqrt(jnp.sum(qb * qb, axis=-1, keepdims=True))
            qh = qb + qn * aug_ref[0:1, :]
        else:
            qh = qh * gk
        q_ref[:, lo:lo + HEAD_PAD] = qh.astype(BF16)

    c = _rms(proj[:, _W1_CKV:_W1_KPE], gkv_ref[...])
    c_ref[...] = c
    kpe = proj[:, _W1_KPE:_W1_KPE_SW] * cos + proj[:, _W1_KPE_SW:_W1_COLS] * sin
    kpe_ref[...] = kpe

    if with_keys:
        cb = c.astype(BF16)
        kn = _dot(cb, wk_ref[...])
        for hd in range(N_HEADS):
            lo = hd * HEAD_PAD
            kh = kn[:, lo:lo + HEAD_PAD] + kpe
            kh = kh * lax.rsqrt(jnp.sum(kh * kh, axis=-1, keepdims=True) * (1.0 / QK_HEAD) + EPS) * gk
            k_ref[:, lo:lo + HEAD_PAD] = (kh + aug_ref[1:2, :]).astype(BF16)
        v_ref[...] = (_dot(cb, wv_ref[...]) + jnp.tile(aug_ref[2:3, :], (1, N_HEADS))).astype(BF16)


def _inproj(x, g, w1, gqa, gkv, wq, cos_tab, sin_tab, gq, gk, wk=None, wv=None, aug=None):
    m = x.shape[0]
    tm = _row_tile(m)
    with_keys = wk is not None
    n_tab = cos_tab.shape[0] // tm
    row = lambda i: (i, 0)
    tab = lambda i: (i % n_tab, 0)
    in_specs = [
        pl.BlockSpec((tm, D_MODEL), row),
        _const_spec((1, D_MODEL)),
        _const_spec((D_MODEL, _W1_COLS)),
        _const_spec((1, Q_LORA)),
        _const_spec((1, KV_LORA)),
        _const_spec((Q_LORA, 2 * _QW)),
        pl.BlockSpec((tm, HEAD_PAD), tab),
        pl.BlockSpec((tm, HEAD_PAD), tab),
        _const_spec((1, HEAD_PAD)),
        _const_spec((1, HEAD_PAD)),
    ]
    args = [x, g, w1, gqa, gkv, wq, cos_tab, sin_tab, gq, gk]
    out_shape = [
        jax.ShapeDtypeStruct((m, CONV_DIM), F32),
        jax.ShapeDtypeStruct((m, _QW), BF16),
        jax.ShapeDtypeStruct((m, KV_LORA), F32),
        jax.ShapeDtypeStruct((m, HEAD_PAD), F32),
    ]
    out_specs = [
        pl.BlockSpec((tm, CONV_DIM), row),
        pl.BlockSpec((tm, _QW), row),
        pl.BlockSpec((tm, KV_LORA), row),
        pl.BlockSpec((tm, HEAD_PAD), row),
    ]
    if with_keys:
        in_specs += [_const_spec((KV_LORA, _QW)), _const_spec((KV_LORA, _QW)), _const_spec((SUBLANES, HEAD_PAD))]
        args += [wk, wv, aug]
        out_shape += [jax.ShapeDtypeStruct((m, _QW), BF16), jax.ShapeDtypeStruct((m, _QW), BF16)]
        out_specs += [pl.BlockSpec((tm, _QW), row), pl.BlockSpec((tm, _QW), row)]
    return pl.pallas_call(
        functools.partial(_inproj_kernel, with_keys=with_keys),
        out_shape=out_shape,
        grid=(m // tm,),
        in_specs=in_specs,
        out_specs=out_specs,
        compiler_params=_params(("parallel",)),
        name="inproj_keys" if with_keys else "inproj",
    )(*args)


def _conv_tail(acc, b, g, beta):
    y = acc + b
    mu = jnp.mean(y, axis=-1, keepdims=True)
    d = y - mu
    var = jnp.mean(d * d, axis=-1, keepdims=True)
    z = d * lax.rsqrt(var + EPS) * g + beta
    return (z * jax.nn.sigmoid(z)).astype(BF16)


def _conv_prompt_kernel(cur_ref, prev_ref, w_ref, b_ref, g_ref, beta_ref, o_ref, ext_ref, shift_ref):
    tt = cur_ref.shape[1]
    first = pl.program_id(1) == 0
    prev = prev_ref[0]
    ext_ref[0:CONV_HALO, :] = jnp.where(first, 0.0, prev)
    ext_ref[CONV_HALO:, :] = cur_ref[0]
    span = shift_ref.shape[1]
    for s in range(1, SUBLANES):
        shift_ref[s - 1] = ext_ref[s:s + span, :]
    base = CONV_HALO - (CONV_WIDTH - 1)
    acc = None
    for j in range(CONV_WIDTH):
        s = (base + j) % SUBLANES
        lo = base + j - s
        rows = ext_ref[lo:lo + tt, :] if s == 0 else shift_ref[s - 1, lo:lo + tt, :]
        term = w_ref[j:j + 1, :] * rows
        acc = term if acc is None else acc + term
    o_ref[0] = _conv_tail(acc, b_ref[...], g_ref[...], beta_ref[...])


def _conv_prompt(u, w, b, g, beta):
    nb, t, _ = u.shape
    tt = _row_tile(t, 256)
    per = tt // CONV_HALO
    return pl.pallas_call(
        _conv_prompt_kernel,
        out_shape=jax.ShapeDtypeStruct((nb, t, CONV_DIM), BF16),
        grid=(nb, t // tt),
        in_specs=[
            pl.BlockSpec((1, tt, CONV_DIM), lambda bi, i: (bi, i, 0)),
            pl.BlockSpec((1, CONV_HALO, CONV_DIM), lambda bi, i: (bi, jnp.maximum(i * per - 1, 0), 0)),
            _const_spec((CONV_WIDTH, CONV_DIM)),
            _const_spec((1, CONV_DIM)),
            _const_spec((1, CONV_DIM)),
            _const_spec((1, CONV_DIM)),
        ],
        out_specs=pl.BlockSpec((1, tt, CONV_DIM), lambda bi, i: (bi, i, 0)),
        scratch_shapes=[pltpu.VMEM((tt + CONV_HALO, CONV_DIM), F32),
                        pltpu.VMEM((SUBLANES - 1, tt + CONV_HALO - SUBLANES, CONV_DIM), F32)],
        compiler_params=_params(("parallel", "parallel")),
        name="conv_prompt",
    )(u, u, w, b, g, beta)


def _conv_sample_kernel(ext_ref, w_ref, b_ref, g_ref, beta_ref, o_ref):
    t = o_ref.shape[1]
    acc = w_ref[0:1, :] * ext_ref[:, 0:t, :]
    for j in range(1, CONV_WIDTH):
        acc = acc + w_ref[j:j + 1, :] * ext_ref[:, j:j + t, :]
    o_ref[...] = _conv_tail(acc, b_ref[...], g_ref[...], beta_ref[...])


def _conv_sample(ext, w, b, g, beta):
    nb, rows, _ = ext.shape
    t = rows - (CONV_WIDTH - 1)
    bs = 16 if nb % 16 == 0 else nb
    return pl.pallas_call(
        _conv_sample_kernel,
        out_shape=jax.ShapeDtypeStruct((nb, t, CONV_DIM), BF16),
        grid=(nb // bs,),
        in_specs=[
            pl.BlockSpec((bs, rows, CONV_DIM), lambda i: (i, 0, 0)),
            _const_spec((CONV_WIDTH, CONV_DIM)),
            _const_spec((1, CONV_DIM)),
            _const_spec((1, CONV_DIM)),
            _const_spec((1, CONV_DIM)),
        ],
        out_specs=pl.BlockSpec((bs, t, CONV_DIM), lambda i: (i, 0, 0)),
        compiler_params=_params(("parallel",)),
        name="conv_sample",
    )(ext, w, b, g, beta)


def _prompt_attn_kernel(q_ref, k_ref, v_ref, o_ref, *scratch, tk, bounded):
    if bounded:
        (acc_ref,) = scratch
    else:
        m_ref, acc_ref = scratch
    tq = q_ref.shape[1]
    qi = pl.program_id(1)
    row = lax.broadcasted_iota(jnp.int32, (tq, tk), 0)
    col = lax.broadcasted_iota(jnp.int32, (tq, tk), 1)
    lane = lax.broadcasted_iota(jnp.int32, (tq, HEAD_PAD), 1)

    def step(hd, j, masked):
        lo = hd * HEAD_PAD
        start = pl.multiple_of(j * tk, tk)
        kh = k_ref[0, pl.ds(start, tk), lo:lo + HEAD_PAD]
        vh = v_ref[0, pl.ds(start, tk), lo:lo + HEAD_PAD]
        s = _dot_nt(q_ref[0, :, lo:lo + HEAD_PAD], kh)
        if masked:
            s = jnp.where(col <= row, s, NEG_BIG)
        if bounded:
            acc_ref[hd] += _dot(jnp.exp2(s).astype(BF16), vh)
        else:
            m_old = m_ref[hd]
            m_new = jnp.maximum(m_old, jnp.max(s, axis=-1, keepdims=True))
            acc_ref[hd] = jnp.exp2(m_old - m_new) * acc_ref[hd] + _dot(jnp.exp2(s - m_new).astype(BF16), vh)
            m_ref[hd] = m_new

    if not bounded:
        m_ref[...] = jnp.full(m_ref.shape, NEG_BIG, F32)
    acc_ref[...] = jnp.zeros(acc_ref.shape, F32)

    def body(j, carry):
        for hd in range(N_HEADS):
            step(hd, j, False)
        return carry

    lax.fori_loop(0, qi, body, 0)
    for hd in range(N_HEADS):
        step(hd, qi, True)
    for pair in range(N_HEADS // 2):
        even = acc_ref[2 * pair]
        odd = acc_ref[2 * pair + 1]
        even = even / even[:, V_HEAD:V_HEAD + 1]
        odd = pltpu.roll(odd / odd[:, V_HEAD:V_HEAD + 1], V_HEAD, 1)
        vlo = pair * HEAD_PAD
        o_ref[0, :, vlo:vlo + HEAD_PAD] = jnp.where(lane < V_HEAD, even, odd).astype(BF16)


def _prompt_attn(q, k, v, bounded):
    nb, t, _ = q.shape
    tq = _row_tile(t)
    scratch = [pltpu.VMEM((N_HEADS, tq, HEAD_PAD), F32)]
    if not bounded:
        scratch = [pltpu.VMEM((N_HEADS, tq, 1), F32)] + scratch
    return pl.pallas_call(
        functools.partial(_prompt_attn_kernel, tk=tq, bounded=bounded),
        out_shape=jax.ShapeDtypeStruct((nb, t, N_HEADS * V_HEAD), BF16),
        grid=(nb, t // tq),
        in_specs=[
            pl.BlockSpec((1, tq, _QW), lambda bi, i: (bi, i, 0)),
            pl.BlockSpec((1, t, _QW), lambda bi, i: (bi, 0, 0)),
            pl.BlockSpec((1, t, _QW), lambda bi, i: (bi, 0, 0)),
        ],
        out_specs=pl.BlockSpec((1, tq, N_HEADS * V_HEAD), lambda bi, i: (bi, i, 0)),
        scratch_shapes=scratch,
        compiler_params=_params(("parallel", "arbitrary")),
        name="prompt_attn_bounded" if bounded else "prompt_attn",
    )(q, k, v)


def _qabs_kernel(q_ref, w_ref, o_ref):
    o_ref[0] = _dot(q_ref[...], w_ref[0]).astype(BF16)


def _qabs(q, wukt):
    m = q.shape[0]
    return pl.pallas_call(
        _qabs_kernel,
        out_shape=jax.ShapeDtypeStruct((N_HEADS, m, KV_LORA), BF16),
        grid=(N_HEADS,),
        in_specs=[
            pl.BlockSpec((m, HEAD_PAD), lambda h: (0, h)),
            pl.BlockSpec((1, HEAD_PAD, KV_LORA), lambda h: (h, 0, 0)),
        ],
        out_specs=pl.BlockSpec((1, m, KV_LORA), lambda h: (h, 0, 0)),
        compiler_params=_params(("parallel",)),
        name="qabs",
    )(q, wukt)


def _sample_attn_kernel(pt_ref, qabs_ref, qr_ref, cnew_ref, kpenew_ref, wukt_ref, wuv_ref,
                        ckv_hbm, kpe_hbm, o_ref,
                        cbuf, kbuf, sem, m_ref, l_ref, acc_ref, cpad, wq_all, s_ref, cb_ref,
                        *, layer, pages_per_chunk):
    b = pl.program_id(0)
    nb = pl.num_programs(0)
    n_pages = pt_ref.shape[1]
    n_chunks = n_pages // pages_per_chunk
    t_new = cnew_ref.shape[1]
    rows = N_HEADS * t_new

    def copies(seq, ck, slot):
        out = []
        for p in range(pages_per_chunk):
            page = pt_ref[seq, ck * pages_per_chunk + p]
            dst = pl.ds(p * PAGE_SIZE, PAGE_SIZE)
            out.append(pltpu.make_async_copy(ckv_hbm.at[layer, page], cbuf.at[slot, dst], sem.at[0, slot]))
            out.append(pltpu.make_async_copy(kpe_hbm.at[layer, page], kbuf.at[slot, :, dst], sem.at[1, slot]))
        return out

    @pl.when(b == 0)
    def _():
        for cp in copies(0, 0, 0):
            cp.start()

    m_ref[...] = jnp.full(m_ref.shape, NEG_BIG, F32)
    l_ref[...] = jnp.zeros(l_ref.shape, F32)
    acc_ref[...] = jnp.zeros(acc_ref.shape, F32)

    n_k = N_HEADS * QK_NOPE
    wq_all[0:n_k, :] = wukt_ref[...]
    wq_all[n_k:, :] = qabs_ref[0]
    qr = qr_ref[0]

    def scores(c32, kpt32):
        n = c32.shape[0]
        cb = c32.astype(BF16)
        kt = _dot_nt(wq_all[...], cb)
        ss = jnp.sum((kt[:n_k] * kt[:n_k]).reshape(QK_NOPE, N_HEADS, n), axis=0)
        ss = ss + jnp.sum(kpt32 * kpt32, axis=0, keepdims=True)
        r = lax.rsqrt(ss * (1.0 / QK_HEAD) + EPS)
        s = kt[n_k:] + _dot(qr, kpt32.astype(BF16))
        return (s.reshape(t_new, N_HEADS, n) * r[None, :, :]).reshape(rows, n), cb

    def update(s, cb):
        m_old = m_ref[...]
        m_new = jnp.maximum(m_old, jnp.max(s, axis=-1, keepdims=True))
        alpha = jnp.exp2(m_old - m_new)
        p = jnp.exp2(s - m_new)
        l_ref[...] = alpha * l_ref[...] + jnp.sum(p, axis=-1, keepdims=True)
        acc_ref[...] = alpha * acc_ref[...] + _dot(p.astype(BF16), cb)
        m_ref[...] = m_new

    s_ref[1] = jnp.full(s_ref.shape[1:], 2.0 * NEG_BIG, F32)
    cb_ref[1] = jnp.zeros(cb_ref.shape[1:], BF16)

    def half_step(ck, par):
        last = ck == n_chunks - 1
        nxt_seq = jnp.where(last, b + 1, b)
        nxt_ck = jnp.where(last, 0, ck + 1)

        @pl.when(jnp.logical_or(jnp.logical_not(last), b + 1 < nb))
        def _():
            for cp in copies(nxt_seq, nxt_ck, 1 - par):
                cp.start()

        for cp in copies(b, ck, par):
            cp.wait()
        s_new, cb_new = scores(cbuf[par], kbuf[par])
        update(s_ref[1 - par], cb_ref[1 - par])
        s_ref[par] = s_new
        cb_ref[par] = cb_new

    def pair_body(pr, carry):
        half_step(2 * pr, 0)
        half_step(2 * pr + 1, 1)
        return carry

    lax.fori_loop(0, n_chunks // 2, pair_body, 0)
    update(s_ref[1], cb_ref[1])

    cpad[...] = jnp.zeros(cpad.shape, F32)
    cpad[0:t_new, :] = cnew_ref[0]
    qpos = lax.broadcasted_iota(jnp.int32, (rows, PAGE_SIZE), 0) // N_HEADS
    kpos = lax.broadcasted_iota(jnp.int32, (rows, PAGE_SIZE), 1)
    s_new, cb_new = scores(cpad[...], kpenew_ref[0])
    update(jnp.where(kpos <= qpos, s_new, NEG_BIG), cb_new)

    o_lat = (acc_ref[...] / l_ref[...]).astype(BF16)
    full = _dot(o_lat, wuv_ref[...])
    full = full.reshape(t_new, N_HEADS, N_HEADS * V_HEAD)
    own = (lax.broadcasted_iota(jnp.int32, (N_HEADS, N_HEADS * V_HEAD), 1) // V_HEAD
           == lax.broadcasted_iota(jnp.int32, (N_HEADS, N_HEADS * V_HEAD), 0))
    o_ref[0] = jnp.sum(jnp.where(own[None], full, 0.0), axis=1).astype(BF16)


def _sample_attn(page_table, qabs, qr, c_new, kpe_new, wukt, wuv, cache_ckv, cache_kpe, layer):
    nb, n_pages = page_table.shape
    t_new = c_new.shape[1]
    rows = N_HEADS * t_new
    assert n_pages % 2 == 0, "the sample kernel pipelines cache chunks in pairs"
    ppc = SAMPLE_PAGES_PER_CHUNK if n_pages % (2 * SAMPLE_PAGES_PER_CHUNK) == 0 else n_pages // 2
    chunk = ppc * PAGE_SIZE
    grid_spec = pltpu.PrefetchScalarGridSpec(
        num_scalar_prefetch=1,
        grid=(nb,),
        in_specs=[
            pl.BlockSpec((1, rows, KV_LORA), lambda b, pt: (b, 0, 0)),
            pl.BlockSpec((1, rows, QK_ROPE), lambda b, pt: (b, 0, 0)),
            pl.BlockSpec((1, t_new, KV_LORA), lambda b, pt: (b, 0, 0)),
            pl.BlockSpec((1, QK_ROPE, PAGE_SIZE), lambda b, pt: (b, 0, 0)),
            pl.BlockSpec((N_HEADS * QK_NOPE, KV_LORA), lambda b, pt: (0, 0)),
            pl.BlockSpec((KV_LORA, N_HEADS * V_HEAD), lambda b, pt: (0, 0)),
            pl.BlockSpec(memory_space=pl.ANY),
            pl.BlockSpec(memory_space=pl.ANY),
        ],
        out_specs=pl.BlockSpec((1, t_new, N_HEADS * V_HEAD), lambda b, pt: (b, 0, 0)),
        scratch_shapes=[
            pltpu.VMEM((2, chunk, KV_LORA), F32),
            pltpu.VMEM((2, QK_ROPE, chunk), F32),
            pltpu.SemaphoreType.DMA((2, 2)),
            pltpu.VMEM((rows, 1), F32),
            pltpu.VMEM((rows, 1), F32),
            pltpu.VMEM((rows, KV_LORA), F32),
            pltpu.VMEM((PAGE_SIZE, KV_LORA), F32),
            pltpu.VMEM((N_HEADS * QK_NOPE + rows, KV_LORA), BF16),
            pltpu.VMEM((2, rows, chunk), F32),
            pltpu.VMEM((2, chunk, KV_LORA), BF16),
        ],
    )
    return pl.pallas_call(
        functools.partial(_sample_attn_kernel, layer=layer, pages_per_chunk=ppc),
        out_shape=jax.ShapeDtypeStruct((nb, t_new, N_HEADS * V_HEAD), BF16),
        grid_spec=grid_spec,
        compiler_params=_params(("arbitrary",)),
        name="sample_attn",
    )(page_table, qabs, qr, c_new, kpe_new, wukt, wuv, cache_ckv, cache_kpe)


def _merge_kernel(x_ref, g_ref, a_ref, o_ref, wg_ref, wc_ref, wa_ref, wo_ref, y_ref):
    x = x_ref[...]
    h = _rms(x, g_ref[...]).astype(BF16)
    gates = jax.nn.sigmoid(_dot(h, wg_ref[...]))
    y_conv = _dot(a_ref[...], wc_ref[...])
    y_attn = _dot(o_ref[...], wa_ref[...])
    mix = gates[:, :D_MODEL] * y_conv + gates[:, D_MODEL:] * y_attn
    y_ref[...] = x + _dot(mix.astype(BF16), wo_ref[...])


def _merge(x, g, a, o, wg, wc, wa, wo):
    m = x.shape[0]
    tm = _row_tile(m)
    row = lambda i: (i, 0)
    return pl.pallas_call(
        _merge_kernel,
        out_shape=jax.ShapeDtypeStruct((m, D_MODEL), F32),
        grid=(m // tm,),
        in_specs=[
            pl.BlockSpec((tm, D_MODEL), row),
            _const_spec((1, D_MODEL)),
            pl.BlockSpec((tm, CONV_DIM), row),
            pl.BlockSpec((tm, N_HEADS * V_HEAD), row),
            _const_spec((D_MODEL, 2 * D_MODEL)),
            _const_spec((CONV_DIM, D_MODEL)),
            _const_spec((N_HEADS * V_HEAD, D_MODEL)),
            _const_spec((D_MODEL, D_MODEL)),
        ],
        out_specs=pl.BlockSpec((tm, D_MODEL), row),
        compiler_params=_params(("parallel",)),
        name="merge",
    )(x, g, a, o, wg, wc, wa, wo)


def _rotate_half_cols(w):
    half = QK_ROPE // 2
    return jnp.concatenate([-w[..., half:], w[..., :half]], axis=-1)


def _rope_tables(pos):
    inv = ROPE_BASE ** (-jnp.arange(0, QK_ROPE, 2, dtype=F32) / QK_ROPE)
    ang = pos.astype(F32)[:, None] * inv[None, :]
    cos, sin = jnp.cos(ang), jnp.sin(ang)
    n = pos.shape[0]
    pad = jnp.zeros((n, HEAD_PAD - QK_HEAD), F32)
    cos_tab = jnp.concatenate([jnp.ones((n, QK_NOPE), F32), cos, cos, pad], axis=-1)
    sin_tab = jnp.concatenate([jnp.zeros((n, QK_NOPE), F32), sin, sin, pad], axis=-1)
    return cos_tab, sin_tab


def kernel(x_prompt, x_sample, cache_ckv, cache_kpe, state_conv, page_table, norm_ffn1, ffn1_w_gu, ffn1_w_down, norm_mix, w_in, conv_w, conv_b, conv_ln_g, conv_ln_b, w_conv_out, q_a_norm, w_uq, kv_a_norm, w_uk, w_uv, q_head_norm, k_head_norm, w_attn_out, w_out, norm_ffn2, ffn2_w_gu, ffn2_w_down):
    depth = norm_ffn1.shape[0]
    nb, t, _ = x_prompt.shape
    ns, ts, _ = x_sample.shape
    past = page_table.shape[1] * PAGE_SIZE
    mp, ms = nb * t, ns * ts

    cos_p, sin_p = _rope_tables(jnp.arange(t))
    cos_s, sin_s = _rope_tables(past + jnp.arange(ts))
    tms = _row_tile(ms)
    cos_s = jnp.tile(cos_s, (tms // ts, 1))
    sin_s = jnp.tile(sin_s, (tms // ts, 1))

    cache_kpe_t = jnp.swapaxes(cache_kpe, 2, 3)
    yp = x_prompt.reshape(mp, D_MODEL)
    ys = x_sample.reshape(ms, D_MODEL)
    outs = [[] for _ in range(6)]
    for l in range(depth):
        wi = w_in[l]
        w_kpe = wi[:, _W1_KPE:_W1_KPE + QK_ROPE]
        zeros_nope = jnp.zeros((D_MODEL, QK_NOPE), F32)
        zeros_pad = jnp.zeros((D_MODEL, HEAD_PAD - QK_HEAD), F32)
        w1 = jnp.concatenate([wi[:, :_W1_KPE], zeros_nope, w_kpe, zeros_pad,
                              zeros_nope, _rotate_half_cols(w_kpe), zeros_pad], axis=-1).astype(BF16)
        w_gates = wi[:, _W1_KPE + QK_ROPE:].astype(BF16)
        wq3 = w_uq[l].reshape(Q_LORA, N_HEADS, QK_HEAD)
        q_nope, q_rope = wq3[..., :QK_NOPE], wq3[..., QK_NOPE:]
        zq_nope = jnp.zeros_like(q_nope)
        zq_pad = jnp.zeros((Q_LORA, N_HEADS, HEAD_PAD - QK_HEAD), F32)
        wq = jnp.concatenate([
            jnp.concatenate([q_nope, q_rope, zq_pad], axis=-1).reshape(Q_LORA, _QW),
            jnp.concatenate([zq_nope, _rotate_half_cols(q_rope), zq_pad], axis=-1).reshape(Q_LORA, _QW),
        ], axis=-1).astype(BF16)
        wk = jnp.pad(w_uk[l], ((0, 0), (0, 0), (0, HEAD_PAD - QK_NOPE))).reshape(KV_LORA, _QW).astype(BF16)
        wv = w_uv[l].reshape(KV_LORA, N_HEADS * V_HEAD).astype(BF16)
        wv_pad = jnp.pad(w_uv[l], ((0, 0), (0, 0), (0, HEAD_PAD - V_HEAD))).reshape(KV_LORA, _QW).astype(BF16)
        wukt = jnp.transpose(w_uk[l], (2, 1, 0)).reshape(N_HEADS * QK_NOPE, KV_LORA).astype(BF16)
        wukt_pad = jnp.pad(jnp.transpose(w_uk[l], (1, 2, 0)),
                           ((0, 0), (0, HEAD_PAD - QK_NOPE), (0, 0))).astype(BF16)
        gpad = jnp.zeros((HEAD_PAD - QK_HEAD,), F32)
        gq = jnp.concatenate([q_head_norm[l] * (SCALE * LOG2E), gpad])[None, :]
        gk = jnp.concatenate([k_head_norm[l], gpad])[None, :]
        k_bound = SHIFT_SLACK * QK_HEAD ** 0.5 * jnp.max(jnp.abs(gk))
        score_bound = k_bound * QK_HEAD ** 0.5 * jnp.max(jnp.abs(gq))
        lane_id = jnp.arange(HEAD_PAD)
        aug = jnp.zeros((SUBLANES, HEAD_PAD), F32)
        aug = aug.at[0].set(jnp.where(lane_id == QK_HEAD, -k_bound, 0.0))
        aug = aug.at[1].set(jnp.where(lane_id == QK_HEAD, 1.0, 0.0))
        aug = aug.at[2].set(jnp.where(lane_id == V_HEAD, 1.0, 0.0))
        row = lambda v: v[None, :]
        ffn1 = (row(norm_ffn1[l]), ffn1_w_gu[l].astype(BF16), ffn1_w_down[l].astype(BF16))
        ffn2 = (row(norm_ffn2[l]), ffn2_w_gu[l].astype(BF16), ffn2_w_down[l].astype(BF16))
        proj_w = (row(norm_mix[l]), w1, row(q_a_norm[l]), row(kv_a_norm[l]), wq)
        conv_p = (conv_w[l], row(conv_b[l]), row(conv_ln_g[l]), row(conv_ln_b[l]))
        merge_w = (w_gates, w_conv_out[l].astype(BF16), w_attn_out[l].astype(BF16), w_out[l].astype(BF16))

        x1 = _ffn(yp, *ffn1)
        u, q, c, kpe, k, v = _inproj(x1, *proj_w, cos_p, sin_p, gq, gk, wk, wv_pad, aug)
        u3 = u.reshape(nb, t, CONV_DIM)
        a = _conv_prompt(u3, *conv_p).reshape(mp, CONV_DIM)
        o = lax.cond(score_bound < MAX_SCORE_SHIFT,
                     functools.partial(_prompt_attn, bounded=True),
                     functools.partial(_prompt_attn, bounded=False),
                     q.reshape(nb, t, _QW), k.reshape(nb, t, _QW), v.reshape(nb, t, _QW))
        o = o.reshape(mp, N_HEADS * V_HEAD)
        x2 = _merge(x1, row(norm_mix[l]), a, o, *merge_w)
        yp = _ffn(x2, *ffn2)
        outs[0].append(c.reshape(nb, t, KV_LORA))
        outs[1].append(kpe[:, QK_NOPE:QK_HEAD].reshape(nb, t, QK_ROPE))
        outs[2].append(u3[:, t - (CONV_WIDTH - 1):, :])

        x1 = _ffn(ys, *ffn1)
        u, q, c, kpe = _inproj(x1, *proj_w, cos_s, sin_s, gq, gk)
        ext = jnp.concatenate([state_conv[l], u.reshape(ns, ts, CONV_DIM)], axis=1)
        a = _conv_sample(ext, *conv_p).reshape(ms, CONV_DIM)
        c3 = c.reshape(ns, ts, KV_LORA)
        kpe3 = kpe[:, QK_NOPE:QK_HEAD].reshape(ns, ts, QK_ROPE)
        qabs = _qabs(q, wukt_pad)
        qabs = qabs.reshape(N_HEADS, ns, ts, KV_LORA).transpose(1, 2, 0, 3).reshape(ns, ts * N_HEADS, KV_LORA)
        qr = q.reshape(ns, ts, N_HEADS, HEAD_PAD)[..., QK_NOPE:QK_HEAD].reshape(ns, ts * N_HEADS, QK_ROPE)
        kpe_new_t = jnp.pad(kpe3.transpose(0, 2, 1), ((0, 0), (0, 0), (0, PAGE_SIZE - ts)))
        o = _sample_attn(page_table, qabs, qr, c3, kpe_new_t, wukt, wv, cache_ckv, cache_kpe_t, l)
        x2 = _merge(x1, row(norm_mix[l]), a, o.reshape(ms, N_HEADS * V_HEAD), *merge_w)
        ys = _ffn(x2, *ffn2)
        outs[3].append(c3)
        outs[4].append(kpe3)
        outs[5].append(ext[:, ts:, :])

    return (yp.reshape(nb, t, D_MODEL), ys.reshape(ns, ts, D_MODEL),
            jnp.stack(outs[0]), jnp.stack(outs[1]), jnp.stack(outs[2]),
            jnp.stack(outs[3]), jnp.stack(outs[4]), jnp.stack(outs[5]))
```

```python
import functools

import jax
import jax.numpy as jnp
from jax import lax
from jax.experimental import pallas as pl
from jax.experimental.pallas import tpu as pltpu

D_MODEL = 1024
CONV_DIM = 512
CONV_WIDTH = 31
N_HEADS = 8
QK_NOPE = 64
QK_ROPE = 32
QK_HEAD = QK_NOPE + QK_ROPE
V_HEAD = 64
Q_LORA = 384
KV_LORA = 256
ROPE_BASE = 10000.0
SCALE = QK_HEAD ** -0.5
LOG2E = 1.4426950408889634
D_FF = 2816
EPS = 1e-6
PAGE_SIZE = 128

SUBLANES = 8
HEAD_PAD = 128
FF_CHUNK = 256
CONV_HALO = 32
SAMPLE_PAGES_PER_CHUNK = 8
SAMPLE_SLOTS = 4
NEG_BIG = -1e30
SHIFT_SLACK = 1.01
MAX_SCORE_SHIFT = 50.0
VMEM_LIMIT = 56 * 1024 * 1024

BF16 = jnp.bfloat16
F32 = jnp.float32


def _row_tile(m, want=512):
    return want if m % want == 0 else m


def _rms(x, g):
    return x * lax.rsqrt(jnp.mean(x * x, axis=-1, keepdims=True) + EPS) * g


def _dot(a, b):
    return jnp.dot(a, b, preferred_element_type=F32)


def _dot_nt(a, b):
    return lax.dot_general(a, b, (((1,), (1,)), ((), ())), preferred_element_type=F32)


def _const_spec(shape):
    nd = len(shape)
    return pl.BlockSpec(shape, lambda *_: (0,) * nd)


def _params(sem):
    return pltpu.CompilerParams(dimension_semantics=sem, vmem_limit_bytes=VMEM_LIMIT)


def _ffn_kernel(x_ref, g_ref, wgu_ref, wd_ref, o_ref, acc_ref):
    x = x_ref[...]
    h = _rms(x, g_ref[...]).astype(BF16)
    for c in range(D_FF // FF_CHUNK):
        lo = c * FF_CHUNK
        gate = _dot(h, wgu_ref[:, lo:lo + FF_CHUNK])
        up = _dot(h, wgu_ref[:, D_FF + lo:D_FF + lo + FF_CHUNK])
        act = (gate * jax.nn.sigmoid(gate) * up).astype(BF16)
        part = _dot(act, wd_ref[lo:lo + FF_CHUNK, :])
        if c == 0:
            acc_ref[...] = part
        else:
            acc_ref[...] += part
    o_ref[...] = x + 0.5 * acc_ref[...]


def _ffn(x, g, wgu, wd):
    m = x.shape[0]
    tm = _row_tile(m)
    return pl.pallas_call(
        _ffn_kernel,
        out_shape=jax.ShapeDtypeStruct((m, D_MODEL), F32),
        grid=(m // tm,),
        in_specs=[
            pl.BlockSpec((tm, D_MODEL), lambda i: (i, 0)),
            _const_spec((1, D_MODEL)),
            _const_spec((D_MODEL, 2 * D_FF)),
            _const_spec((D_FF, D_MODEL)),
        ],
        out_specs=pl.BlockSpec((tm, D_MODEL), lambda i: (i, 0)),
        scratch_shapes=[pltpu.VMEM((tm, D_MODEL), F32)],
        compiler_params=_params(("parallel",)),
        name="ffn",
    )(x, g, wgu, wd)


_W1_QA = 2 * CONV_DIM
_W1_CKV = _W1_QA + Q_LORA
_W1_KPE = _W1_CKV + KV_LORA
_W1_KPE_SW = _W1_KPE + HEAD_PAD
_W1_COLS = _W1_KPE_SW + HEAD_PAD
_QW = N_HEADS * HEAD_PAD


def _inproj_kernel(x_ref, g_ref, w1_ref, gqa_ref, gkv_ref, wq_ref, cos_ref, sin_ref,
                   gq_ref, gk_ref, *rest, with_keys):
    if with_keys:
        wk_ref, wv_ref, aug_ref, u_ref, q_ref, c_ref, kpe_ref, k_ref, v_ref = rest
    else:
        u_ref, q_ref, c_ref, kpe_ref = rest
    h = _rms(x_ref[...], g_ref[...]).astype(BF16)
    proj = _dot(h, w1_ref[...])
    u_ref[...] = proj[:, :CONV_DIM] * jax.nn.sigmoid(proj[:, CONV_DIM:2 * CONV_DIM])

    cos = cos_ref[...]
    sin = sin_ref[...]
    qa = _rms(proj[:, _W1_QA:_W1_CKV], gqa_ref[...]).astype(BF16)
    qq = _dot(qa, wq_ref[...])
    gq = gq_ref[...]
    gk = gk_ref[...]
    for hd in range(N_HEADS):
        lo = hd * HEAD_PAD
        qh = qq[:, lo:lo + HEAD_PAD] * cos + qq[:, _QW + lo:_QW + lo + HEAD_PAD] * sin
        qh = qh * lax.rsqrt(jnp.sum(qh * qh, axis=-1, keepdims=True) * (1.0 / QK_HEAD) + EPS) * gq
        if with_keys:
            qb = qh.astype(BF16).astype(F32)
            qn = jnp.sqrt(jnp.sum(qb * qb, axis=-1, keepdims=True))
            qh = qb + qn * aug_ref[0:1, :]
        else:
            qh = qh * gk
        q_ref[:, lo:lo + HEAD_PAD] = qh.astype(BF16)

    c = _rms(proj[:, _W1_CKV:_W1_KPE], gkv_ref[...])
    c_ref[...] = c
    kpe = proj[:, _W1_KPE:_W1_KPE_SW] * cos + proj[:, _W1_KPE_SW:_W1_COLS] * sin
    kpe_ref[...] = kpe

    if with_keys:
        cb = c.astype(BF16)
        kn = _dot(cb, wk_ref[...])
        for hd in range(N_HEADS):
            lo = hd * HEAD_PAD
            kh = kn[:, lo:lo + HEAD_PAD] + kpe
            kh = kh * lax.rsqrt(jnp.sum(kh * kh, axis=-1, keepdims=True) * (1.0 / QK_HEAD) + EPS) * gk
            k_ref[:, lo:lo + HEAD_PAD] = (kh + aug_ref[1:2, :]).astype(BF16)
        v_ref[...] = (_dot(cb, wv_ref[...]) + jnp.tile(aug_ref[2:3, :], (1, N_HEADS))).astype(BF16)


def _inproj(x, g, w1, gqa, gkv, wq, cos_tab, sin_tab, gq, gk, wk=None, wv=None, aug=None):
    m = x.shape[0]
    tm = _row_tile(m)
    with_keys = wk is not None
    n_tab = cos_tab.shape[0] // tm
    row = lambda i: (i, 0)
    tab = lambda i: (i % n_tab, 0)
    in_specs = [
        pl.BlockSpec((tm, D_MODEL), row),
        _const_spec((1, D_MODEL)),
        _const_spec((D_MODEL, _W1_COLS)),
        _const_spec((1, Q_LORA)),
        _const_spec((1, KV_LORA)),
        _const_spec((Q_LORA, 2 * _QW)),
        pl.BlockSpec((tm, HEAD_PAD), tab),
        pl.BlockSpec((tm, HEAD_PAD), tab),
        _const_spec((1, HEAD_PAD)),
        _const_spec((1, HEAD_PAD)),
    ]
    args = [x, g, w1, gqa, gkv, wq, cos_tab, sin_tab, gq, gk]
    out_shape = [
        jax.ShapeDtypeStruct((m, CONV_DIM), F32),
        jax.ShapeDtypeStruct((m, _QW), BF16),
        jax.ShapeDtypeStruct((m, KV_LORA), F32),
        jax.ShapeDtypeStruct((m, HEAD_PAD), F32),
    ]
    out_specs = [
        pl.BlockSpec((tm, CONV_DIM), row),
        pl.BlockSpec((tm, _QW), row),
        pl.BlockSpec((tm, KV_LORA), row),
        pl.BlockSpec((tm, HEAD_PAD), row),
    ]
    if with_keys:
        in_specs += [_const_spec((KV_LORA, _QW)), _const_spec((KV_LORA, _QW)), _const_spec((SUBLANES, HEAD_PAD))]
        args += [wk, wv, aug]
        out_shape += [jax.ShapeDtypeStruct((m, _QW), BF16), jax.ShapeDtypeStruct((m, _QW), BF16)]
        out_specs += [pl.BlockSpec((tm, _QW), row), pl.BlockSpec((tm, _QW), row)]
    return pl.pallas_call(
        functools.partial(_inproj_kernel, with_keys=with_keys),
        out_shape=out_shape,
        grid=(m // tm,),
        in_specs=in_specs,
        out_specs=out_specs,
        compiler_params=_params(("parallel",)),
        name="inproj_keys" if with_keys else "inproj",
    )(*args)


def _conv_tail(acc, b, g, beta):
    y = acc + b
    mu = jnp.mean(y, axis=-1, keepdims=True)
    d = y - mu
    var = jnp.mean(d * d, axis=-1, keepdims=True)
    z = d * lax.rsqrt(var + EPS) * g + beta
    return (z * jax.nn.sigmoid(z)).astype(BF16)


def _conv_prompt_kernel(cur_ref, prev_ref, w_ref, b_ref, g_ref, beta_ref, o_ref, ext_ref, shift_ref):
    tt = cur_ref.shape[1]
    first = pl.program_id(1) == 0
    prev = prev_ref[0]
    ext_ref[0:CONV_HALO, :] = jnp.where(first, 0.0, prev)
    ext_ref[CONV_HALO:, :] = cur_ref[0]
    span = shift_ref.shape[1]
    for s in range(1, SUBLANES):
        shift_ref[s - 1] = ext_ref[s:s + span, :]
    base = CONV_HALO - (CONV_WIDTH - 1)
    acc = None
    for j in range(CONV_WIDTH):
        s = (base + j) % SUBLANES
        lo = base + j - s
        rows = ext_ref[lo:lo + tt, :] if s == 0 else shift_ref[s - 1, lo:lo + tt, :]
        term = w_ref[j:j + 1, :] * rows
        acc = term if acc is None else acc + term
    o_ref[0] = _conv_tail(acc, b_ref[...], g_ref[...], beta_ref[...])


def _conv_prompt(u, w, b, g, beta):
    nb, t, _ = u.shape
    tt = _row_tile(t, 256)
    per = tt // CONV_HALO
    return pl.pallas_call(
        _conv_prompt_kernel,
        out_shape=jax.ShapeDtypeStruct((nb, t, CONV_DIM), BF16),
        grid=(nb, t // tt),
        in_specs=[
            pl.BlockSpec((1, tt, CONV_DIM), lambda bi, i: (bi, i, 0)),
            pl.BlockSpec((1, CONV_HALO, CONV_DIM), lambda bi, i: (bi, jnp.maximum(i * per - 1, 0), 0)),
            _const_spec((CONV_WIDTH, CONV_DIM)),
            _const_spec((1, CONV_DIM)),
            _const_spec((1, CONV_DIM)),
            _const_spec((1, CONV_DIM)),
        ],
        out_specs=pl.BlockSpec((1, tt, CONV_DIM), lambda bi, i: (bi, i, 0)),
        scratch_shapes=[pltpu.VMEM((tt + CONV_HALO, CONV_DIM), F32),
                        pltpu.VMEM((SUBLANES - 1, tt + CONV_HALO - SUBLANES, CONV_DIM), F32)],
        compiler_params=_params(("parallel", "parallel")),
        name="conv_prompt",
    )(u, u, w, b, g, beta)


def _conv_sample_kernel(ext_ref, w_ref, b_ref, g_ref, beta_ref, o_ref):
    t = o_ref.shape[1]
    acc = w_ref[0:1, :] * ext_ref[:, 0:t, :]
    for j in range(1, CONV_WIDTH):
        acc = acc + w_ref[j:j + 1, :] * ext_ref[:, j:j + t, :]
    o_ref[...] = _conv_tail(acc, b_ref[...], g_ref[...], beta_ref[...])


def _conv_sample(ext, w, b, g, beta):
    nb, rows, _ = ext.shape
    t = rows - (CONV_WIDTH - 1)
    bs = 16 if nb % 16 == 0 else nb
    return pl.pallas_call(
        _conv_sample_kernel,
        out_shape=jax.ShapeDtypeStruct((nb, t, CONV_DIM), BF16),
        grid=(nb // bs,),
        in_specs=[
            pl.BlockSpec((bs, rows, CONV_DIM), lambda i: (i, 0, 0)),
            _const_spec((CONV_WIDTH, CONV_DIM)),
            _const_spec((1, CONV_DIM)),
            _const_spec((1, CONV_DIM)),
            _const_spec((1, CONV_DIM)),
        ],
        out_specs=pl.BlockSpec((bs, t, CONV_DIM), lambda i: (i, 0, 0)),
        compiler_params=_params(("parallel",)),
        name="conv_sample",
    )(ext, w, b, g, beta)


def _prompt_attn_kernel(q_ref, k_ref, v_ref, o_ref, *scratch, tk, bounded):
    if bounded:
        (acc_ref,) = scratch
    else:
        m_ref, acc_ref = scratch
    tq = q_ref.shape[1]
    qi = pl.program_id(1)
    row = lax.broadcasted_iota(jnp.int32, (tq, tk), 0)
    col = lax.broadcasted_iota(jnp.int32, (tq, tk), 1)
    lane = lax.broadcasted_iota(jnp.int32, (tq, HEAD_PAD), 1)

    def step(hd, j, masked):
        lo = hd * HEAD_PAD
        start = pl.multiple_of(j * tk, tk)
        kh = k_ref[0, pl.ds(start, tk), lo:lo + HEAD_PAD]
        vh = v_ref[0, pl.ds(start, tk), lo:lo + HEAD_PAD]
        s = _dot_nt(q_ref[0, :, lo:lo + HEAD_PAD], kh)
        if masked:
            s = jnp.where(col <= row, s, NEG_BIG)
        if bounded:
            acc_ref[hd] += _dot(jnp.exp2(s).astype(BF16), vh)
        else:
            m_old = m_ref[hd]
            m_new = jnp.maximum(m_old, jnp.max(s, axis=-1, keepdims=True))
            acc_ref[hd] = jnp.exp2(m_old - m_new) * acc_ref[hd] + _dot(jnp.exp2(s - m_new).astype(BF16), vh)
            m_ref[hd] = m_new

    if not bounded:
        m_ref[...] = jnp.full(m_ref.shape, NEG_BIG, F32)
    acc_ref[...] = jnp.zeros(acc_ref.shape, F32)

    def body(j, carry):
        for hd in range(N_HEADS):
            step(hd, j, False)
        return carry

    lax.fori_loop(0, qi, body, 0)
    for hd in range(N_HEADS):
        step(hd, qi, True)
    for pair in range(N_HEADS // 2):
        even = acc_ref[2 * pair]
        odd = acc_ref[2 * pair + 1]
        even = even / even[:, V_HEAD:V_HEAD + 1]
        odd = pltpu.roll(odd / odd[:, V_HEAD:V_HEAD + 1], V_HEAD, 1)
        vlo = pair * HEAD_PAD
        o_ref[0, :, vlo:vlo + HEAD_PAD] = jnp.where(lane < V_HEAD, even, odd).astype(BF16)


def _prompt_attn(q, k, v, bounded):
    nb, t, _ = q.shape
    tq = _row_tile(t)
    scratch = [pltpu.VMEM((N_HEADS, tq, HEAD_PAD), F32)]
    if not bounded:
        scratch = [pltpu.VMEM((N_HEADS, tq, 1), F32)] + scratch
    return pl.pallas_call(
        functools.partial(_prompt_attn_kernel, tk=tq, bounded=bounded),
        out_shape=jax.ShapeDtypeStruct((nb, t, N_HEADS * V_HEAD), BF16),
        grid=(nb, t // tq),
        in_specs=[
            pl.BlockSpec((1, tq, _QW), lambda bi, i: (bi, i, 0)),
            pl.BlockSpec((1, t, _QW), lambda bi, i: (bi, 0, 0)),
            pl.BlockSpec((1, t, _QW), lambda bi, i: (bi, 0, 0)),
        ],
        out_specs=pl.BlockSpec((1, tq, N_HEADS * V_HEAD), lambda bi, i: (bi, i, 0)),
        scratch_shapes=scratch,
        compiler_params=_params(("parallel", "arbitrary")),
        name="prompt_attn_bounded" if bounded else "prompt_attn",
    )(q, k, v)


def _qabs_kernel(q_ref, w_ref, o_ref):
    o_ref[0] = _dot(q_ref[...], w_ref[0]).astype(BF16)


def _qabs(q, wukt):
    m = q.shape[0]
    return pl.pallas_call(
        _qabs_kernel,
        out_shape=jax.ShapeDtypeStruct((N_HEADS, m, KV_LORA), BF16),
        grid=(N_HEADS,),
        in_specs=[
            pl.BlockSpec((m, HEAD_PAD), lambda h: (0, h)),
            pl.BlockSpec((1, HEAD_PAD, KV_LORA), lambda h: (h, 0, 0)),
        ],
        out_specs=pl.BlockSpec((1, m, KV_LORA), lambda h: (h, 0, 0)),
        compiler_params=_params(("parallel",)),
        name="qabs",
    )(q, wukt)


def _sample_attn_kernel(pt_ref, qabs_ref, qr_ref, cnew_ref, kpenew_ref, wukt_ref, wuv_ref,
                        ckv_hbm, kpe_hbm, o_ref,
                        cbuf, kbuf, sem, m_ref, l_ref, acc_ref, cpad, wq_all, s_ref, cb_ref,
                        *, layer, pages_per_chunk, n_seq):
    b = pl.program_id(0)
    n_pages = pt_ref.shape[1]
    n_chunks = n_pages // pages_per_chunk
    t_new = cnew_ref.shape[1]
    rows = N_HEADS * t_new

    n_slots = cbuf.shape[0]
    lookahead = n_slots - 1
    total = n_seq * n_chunks

    def copies(g):
        seq = g // n_chunks
        ck = g % n_chunks
        slot = g % n_slots
        out = []
        for p in range(pages_per_chunk):
            page = pt_ref[seq, ck * pages_per_chunk + p]
            dst = pl.ds(p * PAGE_SIZE, PAGE_SIZE)
            out.append(pltpu.make_async_copy(ckv_hbm.at[layer, page], cbuf.at[slot, dst], sem.at[0, slot]))
            out.append(pltpu.make_async_copy(kpe_hbm.at[layer, page], kbuf.at[slot, :, dst], sem.at[1, slot]))
        return out

    @pl.when(b == 0)
    def _():
        for g0 in range(min(lookahead, total)):
            for cp in copies(g0):
                cp.start()

    m_ref[...] = jnp.full(m_ref.shape, NEG_BIG, F32)
    l_ref[...] = jnp.zeros(l_ref.shape, F32)
    acc_ref[...] = jnp.zeros(acc_ref.shape, F32)

    n_k = N_HEADS * QK_NOPE
    wq_all[0:n_k, :] = wukt_ref[...]
    wq_all[n_k:, :] = qabs_ref[0]
    qr = qr_ref[0]

    def scores(c32, kpt32):
        n = c32.shape[0]
        cb = c32.astype(BF16)
        kt = _dot_nt(wq_all[...], cb)
        ss = jnp.sum((kt[:n_k] * kt[:n_k]).reshape(QK_NOPE, N_HEADS, n), axis=0)
        ss = ss + jnp.sum(kpt32 * kpt32, axis=0, keepdims=True)
        r = lax.rsqrt(ss * (1.0 / QK_HEAD) + EPS)
        s = kt[n_k:] + _dot(qr, kpt32.astype(BF16))
        return (s.reshape(t_new, N_HEADS, n) * r[None, :, :]).reshape(rows, n), cb

    def update(s, cb):
        m_old = m_ref[...]
        m_new = jnp.maximum(m_old, jnp.max(s, axis=-1, keepdims=True))
        alpha = jnp.exp2(m_old - m_new)
        p = jnp.exp2(s - m_new)
        l_ref[...] = alpha * l_ref[...] + jnp.sum(p, axis=-1, keepdims=True)
        acc_ref[...] = alpha * acc_ref[...] + _dot(p.astype(BF16), cb)
        m_ref[...] = m_new

    s_ref[1] = jnp.full(s_ref.shape[1:], 2.0 * NEG_BIG, F32)
    cb_ref[1] = jnp.zeros(cb_ref.shape[1:], BF16)

    def half_step(ck, par):
        g = b * n_chunks + ck
        slot = g % n_slots

        @pl.when(g + lookahead < total)
        def _():
            for cp in copies(g + lookahead):
                cp.start()

        for cp in copies(g):
            cp.wait()
        s_new, cb_new = scores(cbuf[slot], kbuf[slot])
        update(s_ref[1 - par], cb_ref[1 - par])
        s_ref[par] = s_new
        cb_ref[par] = cb_new

    def pair_body(pr, carry):
        half_step(2 * pr, 0)
        half_step(2 * pr + 1, 1)
        return carry

    lax.fori_loop(0, n_chunks // 2, pair_body, 0)
    update(s_ref[1], cb_ref[1])

    cpad[...] = jnp.zeros(cpad.shape, F32)
    cpad[0:t_new, :] = cnew_ref[0]
    qpos = lax.broadcasted_iota(jnp.int32, (rows, PAGE_SIZE), 0) // N_HEADS
    kpos = lax.broadcasted_iota(jnp.int32, (rows, PAGE_SIZE), 1)
    s_new, cb_new = scores(cpad[...], kpenew_ref[0])
    update(jnp.where(kpos <= qpos, s_new, NEG_BIG), cb_new)

    o_lat = (acc_ref[...] / l_ref[...]).astype(BF16)
    full = _dot(o_lat, wuv_ref[...])
    full = full.reshape(t_new, N_HEADS, N_HEADS * V_HEAD)
    own = (lax.broadcasted_iota(jnp.int32, (N_HEADS, N_HEADS * V_HEAD), 1) // V_HEAD
           == lax.broadcasted_iota(jnp.int32, (N_HEADS, N_HEADS * V_HEAD), 0))
    o_ref[0] = jnp.sum(jnp.where(own[None], full, 0.0), axis=1).astype(BF16)


def _sample_attn(page_table, qabs, qr, c_new, kpe_new, wukt, wuv, cache_ckv, cache_kpe, layer):
    nb, n_pages = page_table.shape
    t_new = c_new.shape[1]
    rows = N_HEADS * t_new
    assert n_pages % 2 == 0, "the sample kernel pipelines cache chunks in pairs"
    ppc = SAMPLE_PAGES_PER_CHUNK if n_pages % (2 * SAMPLE_PAGES_PER_CHUNK) == 0 else n_pages // 2
    chunk = ppc * PAGE_SIZE
    grid_spec = pltpu.PrefetchScalarGridSpec(
        num_scalar_prefetch=1,
        grid=(nb,),
        in_specs=[
            pl.BlockSpec((1, rows, KV_LORA), lambda b, pt: (b, 0, 0)),
            pl.BlockSpec((1, rows, QK_ROPE), lambda b, pt: (b, 0, 0)),
            pl.BlockSpec((1, t_new, KV_LORA), lambda b, pt: (b, 0, 0)),
            pl.BlockSpec((1, QK_ROPE, PAGE_SIZE), lambda b, pt: (b, 0, 0)),
            pl.BlockSpec((N_HEADS * QK_NOPE, KV_LORA), lambda b, pt: (0, 0)),
            pl.BlockSpec((KV_LORA, N_HEADS * V_HEAD), lambda b, pt: (0, 0)),
            pl.BlockSpec(memory_space=pl.ANY),
            pl.BlockSpec(memory_space=pl.ANY),
        ],
        out_specs=pl.BlockSpec((1, t_new, N_HEADS * V_HEAD), lambda b, pt: (b, 0, 0)),
        scratch_shapes=[
            pltpu.VMEM((SAMPLE_SLOTS, chunk, KV_LORA), F32),
            pltpu.VMEM((SAMPLE_SLOTS, QK_ROPE, chunk), F32),
            pltpu.SemaphoreType.DMA((2, SAMPLE_SLOTS)),
            pltpu.VMEM((rows, 1), F32),
            pltpu.VMEM((rows, 1), F32),
            pltpu.VMEM((rows, KV_LORA), F32),
            pltpu.VMEM((PAGE_SIZE, KV_LORA), F32),
            pltpu.VMEM((N_HEADS * QK_NOPE + rows, KV_LORA), BF16),
            pltpu.VMEM((2, rows, chunk), F32),
            pltpu.VMEM((2, chunk, KV_LORA), BF16),
        ],
    )
    return pl.pallas_call(
        functools.partial(_sample_attn_kernel, layer=layer, pages_per_chunk=ppc, n_seq=nb),
        out_shape=jax.ShapeDtypeStruct((nb, t_new, N_HEADS * V_HEAD), BF16),
        grid_spec=grid_spec,
        compiler_params=_params(("arbitrary",)),
        name="sample_attn",
    )(page_table, qabs, qr, c_new, kpe_new, wukt, wuv, cache_ckv, cache_kpe)


def _merge_kernel(x_ref, g_ref, a_ref, o_ref, wg_ref, wc_ref, wa_ref, wo_ref, y_ref):
    x = x_ref[...]
    h = _rms(x, g_ref[...]).astype(BF16)
    gates = jax.nn.sigmoid(_dot(h, wg_ref[...]))
    y_conv = _dot(a_ref[...], wc_ref[...])
    y_attn = _dot(o_ref[...], wa_ref[...])
    mix = gates[:, :D_MODEL] * y_conv + gates[:, D_MODEL:] * y_attn
    y_ref[...] = x + _dot(mix.astype(BF16), wo_ref[...])


def _merge(x, g, a, o, wg, wc, wa, wo):
    m = x.shape[0]
    tm = _row_tile(m)
    row = lambda i: (i, 0)
    return pl.pallas_call(
        _merge_kernel,
        out_shape=jax.ShapeDtypeStruct((m, D_MODEL), F32),
        grid=(m // tm,),
        in_specs=[
            pl.BlockSpec((tm, D_MODEL), row),
            _const_spec((1, D_MODEL)),
            pl.BlockSpec((tm, CONV_DIM), row),
            pl.BlockSpec((tm, N_HEADS * V_HEAD), row),
            _const_spec((D_MODEL, 2 * D_MODEL)),
            _const_spec((CONV_DIM, D_MODEL)),
            _const_spec((N_HEADS * V_HEAD, D_MODEL)),
            _const_spec((D_MODEL, D_MODEL)),
        ],
        out_specs=pl.BlockSpec((tm, D_MODEL), row),
        compiler_params=_params(("parallel",)),
        name="merge",
    )(x, g, a, o, wg, wc, wa, wo)


def _rotate_half_cols(w):
    half = QK_ROPE // 2
    return jnp.concatenate([-w[..., half:], w[..., :half]], axis=-1)


def _rope_tables(pos):
    inv = ROPE_BASE ** (-jnp.arange(0, QK_ROPE, 2, dtype=F32) / QK_ROPE)
    ang = pos.astype(F32)[:, None] * inv[None, :]
    cos, sin = jnp.cos(ang), jnp.sin(ang)
    n = pos.shape[0]
    pad = jnp.zeros((n, HEAD_PAD - QK_HEAD), F32)
    cos_tab = jnp.concatenate([jnp.ones((n, QK_NOPE), F32), cos, cos, pad], axis=-1)
    sin_tab = jnp.concatenate([jnp.zeros((n, QK_NOPE), F32), sin, sin, pad], axis=-1)
    return cos_tab, sin_tab


def kernel(x_prompt, x_sample, cache_ckv, cache_kpe, state_conv, page_table, norm_ffn1, ffn1_w_gu, ffn1_w_down, norm_mix, w_in, conv_w, conv_b, conv_ln_g, conv_ln_b, w_conv_out, q_a_norm, w_uq, kv_a_norm, w_uk, w_uv, q_head_norm, k_head_norm, w_attn_out, w_out, norm_ffn2, ffn2_w_gu, ffn2_w_down):
    depth = norm_ffn1.shape[0]
    nb, t, _ = x_prompt.shape
    ns, ts, _ = x_sample.shape
    past = page_table.shape[1] * PAGE_SIZE
    mp, ms = nb * t, ns * ts

    cos_p, sin_p = _rope_tables(jnp.arange(t))
    cos_s, sin_s = _rope_tables(past + jnp.arange(ts))
    tms = _row_tile(ms)
    cos_s = jnp.tile(cos_s, (tms // ts, 1))
    sin_s = jnp.tile(sin_s, (tms // ts, 1))

    cache_kpe_t = jnp.swapaxes(cache_kpe, 2, 3)
    yp = x_prompt.reshape(mp, D_MODEL)
    ys = x_sample.reshape(ms, D_MODEL)
    outs = [[] for _ in range(6)]
    for l in range(depth):
        wi = w_in[l]
        w_kpe = wi[:, _W1_KPE:_W1_KPE + QK_ROPE]
        zeros_nope = jnp.zeros((D_MODEL, QK_NOPE), F32)
        zeros_pad = jnp.zeros((D_MODEL, HEAD_PAD - QK_HEAD), F32)
        w1 = jnp.concatenate([wi[:, :_W1_KPE], zeros_nope, w_kpe, zeros_pad,
                              zeros_nope, _rotate_half_cols(w_kpe), zeros_pad], axis=-1).astype(BF16)
        w_gates = wi[:, _W1_KPE + QK_ROPE:].astype(BF16)
        wq3 = w_uq[l].reshape(Q_LORA, N_HEADS, QK_HEAD)
        q_nope, q_rope = wq3[..., :QK_NOPE], wq3[..., QK_NOPE:]
        zq_nope = jnp.zeros_like(q_nope)
        zq_pad = jnp.zeros((Q_LORA, N_HEADS, HEAD_PAD - QK_HEAD), F32)
        wq = jnp.concatenate([
            jnp.concatenate([q_nope, q_rope, zq_pad], axis=-1).reshape(Q_LORA, _QW),
            jnp.concatenate([zq_nope, _rotate_half_cols(q_rope), zq_pad], axis=-1).reshape(Q_LORA, _QW),
        ], axis=-1).astype(BF16)
        wk = jnp.pad(w_uk[l], ((0, 0), (0, 0), (0, HEAD_PAD - QK_NOPE))).reshape(KV_LORA, _QW).astype(BF16)
        wv = w_uv[l].reshape(KV_LORA, N_HEADS * V_HEAD).astype(BF16)
        wv_pad = jnp.pad(w_uv[l], ((0, 0), (0, 0), (0, HEAD_PAD - V_HEAD))).reshape(KV_LORA, _QW).astype(BF16)
        wukt = jnp.transpose(w_uk[l], (2, 1, 0)).reshape(N_HEADS * QK_NOPE, KV_LORA).astype(BF16)
        wukt_pad = jnp.pad(jnp.transpose(w_uk[l], (1, 2, 0)),
                           ((0, 0), (0, HEAD_PAD - QK_NOPE), (0, 0))).astype(BF16)
        gpad = jnp.zeros((HEAD_PAD - QK_HEAD,), F32)
        gq = jnp.concatenate([q_head_norm[l] * (SCALE * LOG2E), gpad])[None, :]
        gk = jnp.concatenate([k_head_norm[l], gpad])[None, :]
        k_bound = SHIFT_SLACK * QK_HEAD ** 0.5 * jnp.max(jnp.abs(gk))
        score_bound = k_bound * QK_HEAD ** 0.5 * jnp.max(jnp.abs(gq))
        lane_id = jnp.arange(HEAD_PAD)
        aug = jnp.zeros((SUBLANES, HEAD_PAD), F32)
        aug = aug.at[0].set(jnp.where(lane_id == QK_HEAD, -k_bound, 0.0))
        aug = aug.at[1].set(jnp.where(lane_id == QK_HEAD, 1.0, 0.0))
        aug = aug.at[2].set(jnp.where(lane_id == V_HEAD, 1.0, 0.0))
        row = lambda v: v[None, :]
        ffn1 = (row(norm_ffn1[l]), ffn1_w_gu[l].astype(BF16), ffn1_w_down[l].astype(BF16))
        ffn2 = (row(norm_ffn2[l]), ffn2_w_gu[l].astype(BF16), ffn2_w_down[l].astype(BF16))
        proj_w = (row(norm_mix[l]), w1, row(q_a_norm[l]), row(kv_a_norm[l]), wq)
        conv_p = (conv_w[l], row(conv_b[l]), row(conv_ln_g[l]), row(conv_ln_b[l]))
        merge_w = (w_gates, w_conv_out[l].astype(BF16), w_attn_out[l].astype(BF16), w_out[l].astype(BF16))

        x1 = _ffn(yp, *ffn1)
        u, q, c, kpe, k, v = _inproj(x1, *proj_w, cos_p, sin_p, gq, gk, wk, wv_pad, aug)
        u3 = u.reshape(nb, t, CONV_DIM)
        a = _conv_prompt(u3, *conv_p).reshape(mp, CONV_DIM)
        o = lax.cond(score_bound < MAX_SCORE_SHIFT,
                     functools.partial(_prompt_attn, bounded=True),
                     functools.partial(_prompt_attn, bounded=False),
                     q.reshape(nb, t, _QW), k.reshape(nb, t, _QW), v.reshape(nb, t, _QW))
        o = o.reshape(mp, N_HEADS * V_HEAD)
        x2 = _merge(x1, row(norm_mix[l]), a, o, *merge_w)
        yp = _ffn(x2, *ffn2)
        outs[0].append(c.reshape(nb, t, KV_LORA))
        outs[1].append(kpe[:, QK_NOPE:QK_HEAD].reshape(nb, t, QK_ROPE))
        outs[2].append(u3[:, t - (CONV_WIDTH - 1):, :])

        x1 = _ffn(ys, *ffn1)
        u, q, c, kpe = _inproj(x1, *proj_w, cos_s, sin_s, gq, gk)
        ext = jnp.concatenate([state_conv[l], u.reshape(ns, ts, CONV_DIM)], axis=1)
        a = _conv_sample(ext, *conv_p).reshape(ms, CONV_DIM)
        c3 = c.reshape(ns, ts, KV_LORA)
        kpe3 = kpe[:, QK_NOPE:QK_HEAD].reshape(ns, ts, QK_ROPE)
        qabs = _qabs(q, wukt_pad)
        qabs = qabs.reshape(N_HEADS, ns, ts, KV_LORA).transpose(1, 2, 0, 3).reshape(ns, ts * N_HEADS, KV_LORA)
        qr = q.reshape(ns, ts, N_HEADS, HEAD_PAD)[..., QK_NOPE:QK_HEAD].reshape(ns, ts * N_HEADS, QK_ROPE)
        kpe_new_t = jnp.pad(kpe3.transpose(0, 2, 1), ((0, 0), (0, 0), (0, PAGE_SIZE - ts)))
        o = _sample_attn(page_table, qabs, qr, c3, kpe_new_t, wukt, wv, cache_ckv, cache_kpe_t, l)
        x2 = _merge(x1, row(norm_mix[l]), a, o.reshape(ms, N_HEADS * V_HEAD), *merge_w)
        ys = _ffn(x2, *ffn2)
        outs[3].append(c3)
        outs[4].append(kpe3)
        outs[5].append(ext[:, ts:, :])

    return (yp.reshape(nb, t, D_MODEL), ys.reshape(ns, ts, D_MODEL),
            jnp.stack(outs[0]), jnp.stack(outs[1]), jnp.stack(outs[2]),
            jnp.stack(outs[3]), jnp.stack(outs[4]), jnp.stack(outs[5]))
```

```python
import functools

import jax
import jax.numpy as jnp
from jax import lax
from jax.experimental import pallas as pl
from jax.experimental.pallas import tpu as pltpu

D_MODEL = 1024
CONV_DIM = 512
CONV_WIDTH = 31
N_HEADS = 8
QK_NOPE = 64
QK_ROPE = 32
QK_HEAD = QK_NOPE + QK_ROPE
V_HEAD = 64
Q_LORA = 384
KV_LORA = 256
ROPE_BASE = 10000.0
SCALE = QK_HEAD ** -0.5
LOG2E = 1.4426950408889634
D_FF = 2816
EPS = 1e-6
PAGE_SIZE = 128

SUBLANES = 8
HEAD_PAD = 128
FF_CHUNK = 256
CONV_HALO = 32
SAMPLE_PAGES_PER_CHUNK = 16
SAMPLE_SLOTS = 4
NEG_BIG = -1e30
SHIFT_SLACK = 1.01
MAX_SCORE_SHIFT = 50.0
VMEM_LIMIT = 56 * 1024 * 1024

BF16 = jnp.bfloat16
F32 = jnp.float32


def _row_tile(m, want=512):
    return want if m % want == 0 else m


def _rms(x, g):
    return x * lax.rsqrt(jnp.mean(x * x, axis=-1, keepdims=True) + EPS) * g


def _dot(a, b):
    return jnp.dot(a, b, preferred_element_type=F32)


def _dot_nt(a, b):
    return lax.dot_general(a, b, (((1,), (1,)), ((), ())), preferred_element_type=F32)


def _const_spec(shape):
    nd = len(shape)
    return pl.BlockSpec(shape, lambda *_: (0,) * nd)


def _params(sem):
    return pltpu.CompilerParams(dimension_semantics=sem, vmem_limit_bytes=VMEM_LIMIT)


def _ffn_kernel(x_ref, g_ref, wgu_ref, wd_ref, o_ref, acc_ref):
    x = x_ref[...]
    h = _rms(x, g_ref[...]).astype(BF16)
    for c in range(D_FF // FF_CHUNK):
        lo = c * FF_CHUNK
        gate = _dot(h, wgu_ref[:, lo:lo + FF_CHUNK])
        up = _dot(h, wgu_ref[:, D_FF + lo:D_FF + lo + FF_CHUNK])
        act = (gate * jax.nn.sigmoid(gate) * up).astype(BF16)
        part = _dot(act, wd_ref[lo:lo + FF_CHUNK, :])
        if c == 0:
            acc_ref[...] = part
        else:
            acc_ref[...] += part
    o_ref[...] = x + 0.5 * acc_ref[...]


def _ffn(x, g, wgu, wd):
    m = x.shape[0]
    tm = _row_tile(m)
    return pl.pallas_call(
        _ffn_kernel,
        out_shape=jax.ShapeDtypeStruct((m, D_MODEL), F32),
        grid=(m // tm,),
        in_specs=[
            pl.BlockSpec((tm, D_MODEL), lambda i: (i, 0)),
            _const_spec((1, D_MODEL)),
            _const_spec((D_MODEL, 2 * D_FF)),
            _const_spec((D_FF, D_MODEL)),
        ],
        out_specs=pl.BlockSpec((tm, D_MODEL), lambda i: (i, 0)),
        scratch_shapes=[pltpu.VMEM((tm, D_MODEL), F32)],
        compiler_params=_params(("parallel",)),
        name="ffn",
    )(x, g, wgu, wd)


_W1_QA = 2 * CONV_DIM
_W1_CKV = _W1_QA + Q_LORA
_W1_KPE = _W1_CKV + KV_LORA
_W1_KPE_SW = _W1_KPE + HEAD_PAD
_W1_COLS = _W1_KPE_SW + HEAD_PAD
_QW = N_HEADS * HEAD_PAD


def _inproj_kernel(x_ref, g_ref, w1_ref, gqa_ref, gkv_ref, wq_ref, cos_ref, sin_ref,
                   gq_ref, gk_ref, *rest, with_keys):
    if with_keys:
        wk_ref, wv_ref, aug_ref, u_ref, q_ref, c_ref, kpe_ref, k_ref, v_ref = rest
    else:
        u_ref, q_ref, c_ref, kpe_ref = rest
    h = _rms(x_ref[...], g_ref[...]).astype(BF16)
    proj = _dot(h, w1_ref[...])
    u_ref[...] = proj[:, :CONV_DIM] * jax.nn.sigmoid(proj[:, CONV_DIM:2 * CONV_DIM])

    cos = cos_ref[...]
    sin = sin_ref[...]
    qa = _rms(proj[:, _W1_QA:_W1_CKV], gqa_ref[...]).astype(BF16)
    qq = _dot(qa, wq_ref[...])
    gq = gq_ref[...]
    gk = gk_ref[...]
    for hd in range(N_HEADS):
        lo = hd * HEAD_PAD
        qh = qq[:, lo:lo + HEAD_PAD] * cos + qq[:, _QW + lo:_QW + lo + HEAD_PAD] * sin
        qh = qh * lax.rsqrt(jnp.sum(qh * qh, axis=-1, keepdims=True) * (1.0 / QK_HEAD) + EPS) * gq
        if with_keys:
            qb = qh.astype(BF16).astype(F32)
            qn = jnp.sqrt(jnp.sum(qb * qb, axis=-1, keepdims=True))
            qh = qb + qn * aug_ref[0:1, :]
        else:
            qh = qh * gk
        q_ref[:, lo:lo + HEAD_PAD] = qh.astype(BF16)

    c = _rms(proj[:, _W1_CKV:_W1_KPE], gkv_ref[...])
    c_ref[...] = c
    kpe = proj[:, _W1_KPE:_W1_KPE_SW] * cos + proj[:, _W1_KPE_SW:_W1_COLS] * sin
    kpe_ref[...] = kpe

    if with_keys:
        cb = c.astype(BF16)
        kn = _dot(cb, wk_ref[...])
        for hd in range(N_HEADS):
            lo = hd * HEAD_PAD
            kh = kn[:, lo:lo + HEAD_PAD] + kpe
            kh = kh * lax.rsqrt(jnp.sum(kh * kh, axis=-1, keepdims=True) * (1.0 / QK_HEAD) + EPS) * gk
            k_ref[:, lo:lo + HEAD_PAD] = (kh + aug_ref[1:2, :]).astype(BF16)
        v_ref[...] = (_dot(cb, wv_ref[...]) + jnp.tile(aug_ref[2:3, :], (1, N_HEADS))).astype(BF16)


def _inproj(x, g, w1, gqa, gkv, wq, cos_tab, sin_tab, gq, gk, wk=None, wv=None, aug=None):
    m = x.shape[0]
    tm = _row_tile(m)
    with_keys = wk is not None
    n_tab = cos_tab.shape[0] // tm
    row = lambda i: (i, 0)
    tab = lambda i: (i % n_tab, 0)
    in_specs = [
        pl.BlockSpec((tm, D_MODEL), row),
        _const_spec((1, D_MODEL)),
        _const_spec((D_MODEL, _W1_COLS)),
        _const_spec((1, Q_LORA)),
        _const_spec((1, KV_LORA)),
        _const_spec((Q_LORA, 2 * _QW)),
        pl.BlockSpec((tm, HEAD_PAD), tab),
        pl.BlockSpec((tm, HEAD_PAD), tab),
        _const_spec((1, HEAD_PAD)),
        _const_spec((1, HEAD_PAD)),
    ]
    args = [x, g, w1, gqa, gkv, wq, cos_tab, sin_tab, gq, gk]
    out_shape = [
        jax.ShapeDtypeStruct((m, CONV_DIM), F32),
        jax.ShapeDtypeStruct((m, _QW), BF16),
        jax.ShapeDtypeStruct((m, KV_LORA), F32),
        jax.ShapeDtypeStruct((m, HEAD_PAD), F32),
    ]
    out_specs = [
        pl.BlockSpec((tm, CONV_DIM), row),
        pl.BlockSpec((tm, _QW), row),
        pl.BlockSpec((tm, KV_LORA), row),
        pl.BlockSpec((tm, HEAD_PAD), row),
    ]
    if with_keys:
        in_specs += [_const_spec((KV_LORA, _QW)), _const_spec((KV_LORA, _QW)), _const_spec((SUBLANES, HEAD_PAD))]
        args += [wk, wv, aug]
        out_shape += [jax.ShapeDtypeStruct((m, _QW), BF16), jax.ShapeDtypeStruct((m, _QW), BF16)]
        out_specs += [pl.BlockSpec((tm, _QW), row), pl.BlockSpec((tm, _QW), row)]
    return pl.pallas_call(
        functools.partial(_inproj_kernel, with_keys=with_keys),
        out_shape=out_shape,
        grid=(m // tm,),
        in_specs=in_specs,
        out_specs=out_specs,
        compiler_params=_params(("parallel",)),
        name="inproj_keys" if with_keys else "inproj",
    )(*args)


def _conv_tail(acc, b, g, beta):
    y = acc + b
    mu = jnp.mean(y, axis=-1, keepdims=True)
    d = y - mu
    var = jnp.mean(d * d, axis=-1, keepdims=True)
    z = d * lax.rsqrt(var + EPS) * g + beta
    return (z * jax.nn.sigmoid(z)).astype(BF16)


def _conv_prompt_kernel(cur_ref, prev_ref, w_ref, b_ref, g_ref, beta_ref, o_ref, ext_ref, shift_ref):
    tt = cur_ref.shape[1]
    first = pl.program_id(1) == 0
    prev = prev_ref[0]
    ext_ref[0:CONV_HALO, :] = jnp.where(first, 0.0, prev)
    ext_ref[CONV_HALO:, :] = cur_ref[0]
    span = shift_ref.shape[1]
    for s in range(1, SUBLANES):
        shift_ref[s - 1] = ext_ref[s:s + span, :]
    base = CONV_HALO - (CONV_WIDTH - 1)
    acc = None
    for j in range(CONV_WIDTH):
        s = (base + j) % SUBLANES
        lo = base + j - s
        rows = ext_ref[lo:lo + tt, :] if s == 0 else shift_ref[s - 1, lo:lo + tt, :]
        term = w_ref[j:j + 1, :] * rows
        acc = term if acc is None else acc + term
    o_ref[0] = _conv_tail(acc, b_ref[...], g_ref[...], beta_ref[...])


def _conv_prompt(u, w, b, g, beta):
    nb, t, _ = u.shape
    tt = _row_tile(t, 256)
    per = tt // CONV_HALO
    return pl.pallas_call(
        _conv_prompt_kernel,
        out_shape=jax.ShapeDtypeStruct((nb, t, CONV_DIM), BF16),
        grid=(nb, t // tt),
        in_specs=[
            pl.BlockSpec((1, tt, CONV_DIM), lambda bi, i: (bi, i, 0)),
            pl.BlockSpec((1, CONV_HALO, CONV_DIM), lambda bi, i: (bi, jnp.maximum(i * per - 1, 0), 0)),
            _const_spec((CONV_WIDTH, CONV_DIM)),
            _const_spec((1, CONV_DIM)),
            _const_spec((1, CONV_DIM)),
            _const_spec((1, CONV_DIM)),
        ],
        out_specs=pl.BlockSpec((1, tt, CONV_DIM), lambda bi, i: (bi, i, 0)),
        scratch_shapes=[pltpu.VMEM((tt + CONV_HALO, CONV_DIM), F32),
                        pltpu.VMEM((SUBLANES - 1, tt + CONV_HALO - SUBLANES, CONV_DIM), F32)],
        compiler_params=_params(("parallel", "parallel")),
        name="conv_prompt",
    )(u, u, w, b, g, beta)


def _conv_sample_kernel(ext_ref, w_ref, b_ref, g_ref, beta_ref, o_ref):
    t = o_ref.shape[1]
    acc = w_ref[0:1, :] * ext_ref[:, 0:t, :]
    for j in range(1, CONV_WIDTH):
        acc = acc + w_ref[j:j + 1, :] * ext_ref[:, j:j + t, :]
    o_ref[...] = _conv_tail(acc, b_ref[...], g_ref[...], beta_ref[...])


def _conv_sample(ext, w, b, g, beta):
    nb, rows, _ = ext.shape
    t = rows - (CONV_WIDTH - 1)
    bs = 16 if nb % 16 == 0 else nb
    return pl.pallas_call(
        _conv_sample_kernel,
        out_shape=jax.ShapeDtypeStruct((nb, t, CONV_DIM), BF16),
        grid=(nb // bs,),
        in_specs=[
            pl.BlockSpec((bs, rows, CONV_DIM), lambda i: (i, 0, 0)),
            _const_spec((CONV_WIDTH, CONV_DIM)),
            _const_spec((1, CONV_DIM)),
            _const_spec((1, CONV_DIM)),
            _const_spec((1, CONV_DIM)),
        ],
        out_specs=pl.BlockSpec((bs, t, CONV_DIM), lambda i: (i, 0, 0)),
        compiler_params=_params(("parallel",)),
        name="conv_sample",
    )(ext, w, b, g, beta)


def _prompt_attn_kernel(q_ref, k_ref, v_ref, o_ref, *scratch, tk, bounded):
    if bounded:
        (acc_ref,) = scratch
    else:
        m_ref, acc_ref = scratch
    tq = q_ref.shape[1]
    qi = pl.program_id(1)
    row = lax.broadcasted_iota(jnp.int32, (tq, tk), 0)
    col = lax.broadcasted_iota(jnp.int32, (tq, tk), 1)
    lane = lax.broadcasted_iota(jnp.int32, (tq, HEAD_PAD), 1)

    def step(hd, j, masked):
        lo = hd * HEAD_PAD
        start = pl.multiple_of(j * tk, tk)
        kh = k_ref[0, pl.ds(start, tk), lo:lo + HEAD_PAD]
        vh = v_ref[0, pl.ds(start, tk), lo:lo + HEAD_PAD]
        s = _dot_nt(q_ref[0, :, lo:lo + HEAD_PAD], kh)
        if masked:
            s = jnp.where(col <= row, s, NEG_BIG)
        if bounded:
            acc_ref[hd] += _dot(jnp.exp2(s).astype(BF16), vh)
        else:
            m_old = m_ref[hd]
            m_new = jnp.maximum(m_old, jnp.max(s, axis=-1, keepdims=True))
            acc_ref[hd] = jnp.exp2(m_old - m_new) * acc_ref[hd] + _dot(jnp.exp2(s - m_new).astype(BF16), vh)
            m_ref[hd] = m_new

    if not bounded:
        m_ref[...] = jnp.full(m_ref.shape, NEG_BIG, F32)
    acc_ref[...] = jnp.zeros(acc_ref.shape, F32)

    def body(j, carry):
        for hd in range(N_HEADS):
            step(hd, j, False)
        return carry

    lax.fori_loop(0, qi, body, 0)
    for hd in range(N_HEADS):
        step(hd, qi, True)
    for pair in range(N_HEADS // 2):
        even = acc_ref[2 * pair]
        odd = acc_ref[2 * pair + 1]
        even = even / even[:, V_HEAD:V_HEAD + 1]
        odd = pltpu.roll(odd / odd[:, V_HEAD:V_HEAD + 1], V_HEAD, 1)
        vlo = pair * HEAD_PAD
        o_ref[0, :, vlo:vlo + HEAD_PAD] = jnp.where(lane < V_HEAD, even, odd).astype(BF16)


def _prompt_attn(q, k, v, bounded):
    nb, t, _ = q.shape
    tq = _row_tile(t)
    scratch = [pltpu.VMEM((N_HEADS, tq, HEAD_PAD), F32)]
    if not bounded:
        scratch = [pltpu.VMEM((N_HEADS, tq, 1), F32)] + scratch
    return pl.pallas_call(
        functools.partial(_prompt_attn_kernel, tk=tq, bounded=bounded),
        out_shape=jax.ShapeDtypeStruct((nb, t, N_HEADS * V_HEAD), BF16),
        grid=(nb, t // tq),
        in_specs=[
            pl.BlockSpec((1, tq, _QW), lambda bi, i: (bi, i, 0)),
            pl.BlockSpec((1, t, _QW), lambda bi, i: (bi, 0, 0)),
            pl.BlockSpec((1, t, _QW), lambda bi, i: (bi, 0, 0)),
        ],
        out_specs=pl.BlockSpec((1, tq, N_HEADS * V_HEAD), lambda bi, i: (bi, i, 0)),
        scratch_shapes=scratch,
        compiler_params=_params(("parallel", "arbitrary")),
        name="prompt_attn_bounded" if bounded else "prompt_attn",
    )(q, k, v)


def _qabs_kernel(q_ref, w_ref, o_ref):
    o_ref[0] = _dot(q_ref[...], w_ref[0]).astype(BF16)


def _qabs(q, wukt):
    m = q.shape[0]
    return pl.pallas_call(
        _qabs_kernel,
        out_shape=jax.ShapeDtypeStruct((N_HEADS, m, KV_LORA), BF16),
        grid=(N_HEADS,),
        in_specs=[
            pl.BlockSpec((m, HEAD_PAD), lambda h: (0, h)),
            pl.BlockSpec((1, HEAD_PAD, KV_LORA), lambda h: (h, 0, 0)),
        ],
        out_specs=pl.BlockSpec((1, m, KV_LORA), lambda h: (h, 0, 0)),
        compiler_params=_params(("parallel",)),
        name="qabs",
    )(q, wukt)


def _sample_attn_kernel(pt_ref, qabs_ref, qr_ref, cnew_ref, kpenew_ref, wukt_ref, wuv_ref,
                        ckv_hbm, kpe_hbm, o_ref,
                        cbuf, kbuf, sem, m_ref, l_ref, acc_ref, cpad, wq_all, s_ref, cb_ref,
                        *, layer, pages_per_chunk, n_seq):
    b = pl.program_id(0)
    n_pages = pt_ref.shape[1]
    n_chunks = n_pages // pages_per_chunk
    t_new = cnew_ref.shape[1]
    rows = N_HEADS * t_new

    n_slots = cbuf.shape[0]
    lookahead = n_slots - 1
    total = n_seq * n_chunks

    def copies(g):
        seq = g // n_chunks
        ck = g % n_chunks
        slot = g % n_slots
        out = []
        for p in range(pages_per_chunk):
            page = pt_ref[seq, ck * pages_per_chunk + p]
            dst = pl.ds(p * PAGE_SIZE, PAGE_SIZE)
            out.append(pltpu.make_async_copy(ckv_hbm.at[layer, page], cbuf.at[slot, dst], sem.at[0, slot]))
            out.append(pltpu.make_async_copy(kpe_hbm.at[layer, page], kbuf.at[slot, :, dst], sem.at[1, slot]))
        return out

    @pl.when(b == 0)
    def _():
        for g0 in range(min(lookahead, total)):
            for cp in copies(g0):
                cp.start()

    m_ref[...] = jnp.full(m_ref.shape, NEG_BIG, F32)
    l_ref[...] = jnp.zeros(l_ref.shape, F32)
    acc_ref[...] = jnp.zeros(acc_ref.shape, F32)

    n_k = N_HEADS * QK_NOPE
    wq_all[0:n_k, :] = wukt_ref[...]
    wq_all[n_k:, :] = qabs_ref[0]
    qr = qr_ref[0]

    def scores(c32, kpt32):
        n = c32.shape[0]
        cb = c32.astype(BF16)
        kt = _dot_nt(wq_all[...], cb)
        ss = jnp.sum((kt[:n_k] * kt[:n_k]).reshape(QK_NOPE, N_HEADS, n), axis=0)
        ss = ss + jnp.sum(kpt32 * kpt32, axis=0, keepdims=True)
        r = lax.rsqrt(ss * (1.0 / QK_HEAD) + EPS)
        s = kt[n_k:] + _dot(qr, kpt32.astype(BF16))
        return (s.reshape(t_new, N_HEADS, n) * r[None, :, :]).reshape(rows, n), cb

    def update(s, cb):
        m_old = m_ref[...]
        m_new = jnp.maximum(m_old, jnp.max(s, axis=-1, keepdims=True))
        alpha = jnp.exp2(m_old - m_new)
        p = jnp.exp2(s - m_new)
        l_ref[...] = alpha * l_ref[...] + jnp.sum(p, axis=-1, keepdims=True)
        acc_ref[...] = alpha * acc_ref[...] + _dot(p.astype(BF16), cb)
        m_ref[...] = m_new

    def half_step(ck, par, fold_previous=True):
        g = b * n_chunks + ck
        slot = g % n_slots

        @pl.when(g + lookahead < total)
        def _():
            for cp in copies(g + lookahead):
                cp.start()

        for cp in copies(g):
            cp.wait()
        s_new, cb_new = scores(cbuf[slot], kbuf[slot])
        if fold_previous:
            update(s_ref[1 - par], cb_ref[1 - par])
        s_ref[par] = s_new
        cb_ref[par] = cb_new

    def pair_body(pr, carry):
        half_step(2 * pr + 1, 1)
        half_step(2 * pr + 2, 0)
        return carry

    half_step(0, 0, fold_previous=False)
    lax.fori_loop(0, n_chunks // 2 - 1, pair_body, 0)
    half_step(n_chunks - 1, 1)
    update(s_ref[1], cb_ref[1])

    cpad[...] = jnp.zeros(cpad.shape, F32)
    cpad[0:t_new, :] = cnew_ref[0]
    qpos = lax.broadcasted_iota(jnp.int32, (rows, PAGE_SIZE), 0) // N_HEADS
    kpos = lax.broadcasted_iota(jnp.int32, (rows, PAGE_SIZE), 1)
    s_new, cb_new = scores(cpad[...], kpenew_ref[0])
    update(jnp.where(kpos <= qpos, s_new, NEG_BIG), cb_new)

    o_lat = (acc_ref[...] / l_ref[...]).astype(BF16)
    full = _dot(o_lat, wuv_ref[...])
    full = full.reshape(t_new, N_HEADS, N_HEADS * V_HEAD)
    own = (lax.broadcasted_iota(jnp.int32, (N_HEADS, N_HEADS * V_HEAD), 1) // V_HEAD
           == lax.broadcasted_iota(jnp.int32, (N_HEADS, N_HEADS * V_HEAD), 0))
    o_ref[0] = jnp.sum(jnp.where(own[None], full, 0.0), axis=1).astype(BF16)


def _sample_attn(page_table, qabs, qr, c_new, kpe_new, wukt, wuv, cache_ckv, cache_kpe, layer):
    nb, n_pages = page_table.shape
    t_new = c_new.shape[1]
    rows = N_HEADS * t_new
    assert n_pages % 2 == 0, "the sample kernel pipelines cache chunks in pairs"
    ppc = SAMPLE_PAGES_PER_CHUNK if n_pages % (2 * SAMPLE_PAGES_PER_CHUNK) == 0 else n_pages // 2
    chunk = ppc * PAGE_SIZE
    grid_spec = pltpu.PrefetchScalarGridSpec(
        num_scalar_prefetch=1,
        grid=(nb,),
        in_specs=[
            pl.BlockSpec((1, rows, KV_LORA), lambda b, pt: (b, 0, 0)),
            pl.BlockSpec((1, rows, QK_ROPE), lambda b, pt: (b, 0, 0)),
            pl.BlockSpec((1, t_new, KV_LORA), lambda b, pt: (b, 0, 0)),
            pl.BlockSpec((1, QK_ROPE, PAGE_SIZE), lambda b, pt: (b, 0, 0)),
            pl.BlockSpec((N_HEADS * QK_NOPE, KV_LORA), lambda b, pt: (0, 0)),
            pl.BlockSpec((KV_LORA, N_HEADS * V_HEAD), lambda b, pt: (0, 0)),
            pl.BlockSpec(memory_space=pl.ANY),
            pl.BlockSpec(memory_space=pl.ANY),
        ],
        out_specs=pl.BlockSpec((1, t_new, N_HEADS * V_HEAD), lambda b, pt: (b, 0, 0)),
        scratch_shapes=[
            pltpu.VMEM((SAMPLE_SLOTS, chunk, KV_LORA), F32),
            pltpu.VMEM((SAMPLE_SLOTS, QK_ROPE, chunk), F32),
            pltpu.SemaphoreType.DMA((2, SAMPLE_SLOTS)),
            pltpu.VMEM((rows, 1), F32),
            pltpu.VMEM((rows, 1), F32),
            pltpu.VMEM((rows, KV_LORA), F32),
            pltpu.VMEM((PAGE_SIZE, KV_LORA), F32),
            pltpu.VMEM((N_HEADS * QK_NOPE + rows, KV_LORA), BF16),
            pltpu.VMEM((2, rows, chunk), F32),
            pltpu.VMEM((2, chunk, KV_LORA), BF16),
        ],
    )
    return pl.pallas_call(
        functools.partial(_sample_attn_kernel, layer=layer, pages_per_chunk=ppc, n_seq=nb),
        out_shape=jax.ShapeDtypeStruct((nb, t_new, N_HEADS * V_HEAD), BF16),
        grid_spec=grid_spec,
        compiler_params=_params(("arbitrary",)),
        name="sample_attn",
    )(page_table, qabs, qr, c_new, kpe_new, wukt, wuv, cache_ckv, cache_kpe)


def _merge_kernel(x_ref, g_ref, a_ref, o_ref, wg_ref, wc_ref, wa_ref, wo_ref, y_ref):
    x = x_ref[...]
    h = _rms(x, g_ref[...]).astype(BF16)
    gates = jax.nn.sigmoid(_dot(h, wg_ref[...]))
    y_conv = _dot(a_ref[...], wc_ref[...])
    y_attn = _dot(o_ref[...], wa_ref[...])
    mix = gates[:, :D_MODEL] * y_conv + gates[:, D_MODEL:] * y_attn
    y_ref[...] = x + _dot(mix.astype(BF16), wo_ref[...])


def _merge(x, g, a, o, wg, wc, wa, wo):
    m = x.shape[0]
    tm = _row_tile(m)
    row = lambda i: (i, 0)
    return pl.pallas_call(
        _merge_kernel,
        out_shape=jax.ShapeDtypeStruct((m, D_MODEL), F32),
        grid=(m // tm,),
        in_specs=[
            pl.BlockSpec((tm, D_MODEL), row),
            _const_spec((1, D_MODEL)),
            pl.BlockSpec((tm, CONV_DIM), row),
            pl.BlockSpec((tm, N_HEADS * V_HEAD), row),
            _const_spec((D_MODEL, 2 * D_MODEL)),
            _const_spec((CONV_DIM, D_MODEL)),
            _const_spec((N_HEADS * V_HEAD, D_MODEL)),
            _const_spec((D_MODEL, D_MODEL)),
        ],
        out_specs=pl.BlockSpec((tm, D_MODEL), row),
        compiler_params=_params(("parallel",)),
        name="merge",
    )(x, g, a, o, wg, wc, wa, wo)


def _rotate_half_cols(w):
    half = QK_ROPE // 2
    return jnp.concatenate([-w[..., half:], w[..., :half]], axis=-1)


def _rope_tables(pos):
    inv = ROPE_BASE ** (-jnp.arange(0, QK_ROPE, 2, dtype=F32) / QK_ROPE)
    ang = pos.astype(F32)[:, None] * inv[None, :]
    cos, sin = jnp.cos(ang), jnp.sin(ang)
    n = pos.shape[0]
    pad = jnp.zeros((n, HEAD_PAD - QK_HEAD), F32)
    cos_tab = jnp.concatenate([jnp.ones((n, QK_NOPE), F32), cos, cos, pad], axis=-1)
    sin_tab = jnp.concatenate([jnp.zeros((n, QK_NOPE), F32), sin, sin, pad], axis=-1)
    return cos_tab, sin_tab


def kernel(x_prompt, x_sample, cache_ckv, cache_kpe, state_conv, page_table, norm_ffn1, ffn1_w_gu, ffn1_w_down, norm_mix, w_in, conv_w, conv_b, conv_ln_g, conv_ln_b, w_conv_out, q_a_norm, w_uq, kv_a_norm, w_uk, w_uv, q_head_norm, k_head_norm, w_attn_out, w_out, norm_ffn2, ffn2_w_gu, ffn2_w_down):
    depth = norm_ffn1.shape[0]
    nb, t, _ = x_prompt.shape
    ns, ts, _ = x_sample.shape
    past = page_table.shape[1] * PAGE_SIZE
    mp, ms = nb * t, ns * ts

    cos_p, sin_p = _rope_tables(jnp.arange(t))
    cos_s, sin_s = _rope_tables(past + jnp.arange(ts))
    tms = _row_tile(ms)
    cos_s = jnp.tile(cos_s, (tms // ts, 1))
    sin_s = jnp.tile(sin_s, (tms // ts, 1))

    cache_kpe_t = jnp.swapaxes(cache_kpe, 2, 3)
    yp = x_prompt.reshape(mp, D_MODEL)
    ys = x_sample.reshape(ms, D_MODEL)
    outs = [[] for _ in range(6)]
    for l in range(depth):
        wi = w_in[l]
        w_kpe = wi[:, _W1_KPE:_W1_KPE + QK_ROPE]
        zeros_nope = jnp.zeros((D_MODEL, QK_NOPE), F32)
        zeros_pad = jnp.zeros((D_MODEL, HEAD_PAD - QK_HEAD), F32)
        w1 = jnp.concatenate([wi[:, :_W1_KPE], zeros_nope, w_kpe, zeros_pad,
                              zeros_nope, _rotate_half_cols(w_kpe), zeros_pad], axis=-1).astype(BF16)
        w_gates = wi[:, _W1_KPE + QK_ROPE:].astype(BF16)
        wq3 = w_uq[l].reshape(Q_LORA, N_HEADS, QK_HEAD)
        q_nope, q_rope = wq3[..., :QK_NOPE], wq3[..., QK_NOPE:]
        zq_nope = jnp.zeros_like(q_nope)
        zq_pad = jnp.zeros((Q_LORA, N_HEADS, HEAD_PAD - QK_HEAD), F32)
        wq = jnp.concatenate([
            jnp.concatenate([q_nope, q_rope, zq_pad], axis=-1).reshape(Q_LORA, _QW),
            jnp.concatenate([zq_nope, _rotate_half_cols(q_rope), zq_pad], axis=-1).reshape(Q_LORA, _QW),
        ], axis=-1).astype(BF16)
        wk = jnp.pad(w_uk[l], ((0, 0), (0, 0), (0, HEAD_PAD - QK_NOPE))).reshape(KV_LORA, _QW).astype(BF16)
        wv = w_uv[l].reshape(KV_LORA, N_HEADS * V_HEAD).astype(BF16)
        wv_pad = jnp.pad(w_uv[l], ((0, 0), (0, 0), (0, HEAD_PAD - V_HEAD))).reshape(KV_LORA, _QW).astype(BF16)
        wukt = jnp.transpose(w_uk[l], (2, 1, 0)).reshape(N_HEADS * QK_NOPE, KV_LORA).astype(BF16)
        wukt_pad = jnp.pad(jnp.transpose(w_uk[l], (1, 2, 0)),
                           ((0, 0), (0, HEAD_PAD - QK_NOPE), (0, 0))).astype(BF16)
        gpad = jnp.zeros((HEAD_PAD - QK_HEAD,), F32)
        gq = jnp.concatenate([q_head_norm[l] * (SCALE * LOG2E), gpad])[None, :]
        gk = jnp.concatenate([k_head_norm[l], gpad])[None, :]
        k_bound = SHIFT_SLACK * QK_HEAD ** 0.5 * jnp.max(jnp.abs(gk))
        score_bound = k_bound * QK_HEAD ** 0.5 * jnp.max(jnp.abs(gq))
        lane_id = jnp.arange(HEAD_PAD)
        aug = jnp.zeros((SUBLANES, HEAD_PAD), F32)
        aug = aug.at[0].set(jnp.where(lane_id == QK_HEAD, -k_bound, 0.0))
        aug = aug.at[1].set(jnp.where(lane_id == QK_HEAD, 1.0, 0.0))
        aug = aug.at[2].set(jnp.where(lane_id == V_HEAD, 1.0, 0.0))
        row = lambda v: v[None, :]
        ffn1 = (row(norm_ffn1[l]), ffn1_w_gu[l].astype(BF16), ffn1_w_down[l].astype(BF16))
        ffn2 = (row(norm_ffn2[l]), ffn2_w_gu[l].astype(BF16), ffn2_w_down[l].astype(BF16))
        proj_w = (row(norm_mix[l]), w1, row(q_a_norm[l]), row(kv_a_norm[l]), wq)
        conv_p = (conv_w[l], row(conv_b[l]), row(conv_ln_g[l]), row(conv_ln_b[l]))
        merge_w = (w_gates, w_conv_out[l].astype(BF16), w_attn_out[l].astype(BF16), w_out[l].astype(BF16))

        x1 = _ffn(yp, *ffn1)
        u, q, c, kpe, k, v = _inproj(x1, *proj_w, cos_p, sin_p, gq, gk, wk, wv_pad, aug)
        u3 = u.reshape(nb, t, CONV_DIM)
        a = _conv_prompt(u3, *conv_p).reshape(mp, CONV_DIM)
        o = lax.cond(score_bound < MAX_SCORE_SHIFT,
                     functools.partial(_prompt_attn, bounded=True),
                     functools.partial(_prompt_attn, bounded=False),
                     q.reshape(nb, t, _QW), k.reshape(nb, t, _QW), v.reshape(nb, t, _QW))
        o = o.reshape(mp, N_HEADS * V_HEAD)
        x2 = _merge(x1, row(norm_mix[l]), a, o, *merge_w)
        yp = _ffn(x2, *ffn2)
        outs[0].append(c.reshape(nb, t, KV_LORA))
        outs[1].append(kpe[:, QK_NOPE:QK_HEAD].reshape(nb, t, QK_ROPE))
        outs[2].append(u3[:, t - (CONV_WIDTH - 1):, :])

        x1 = _ffn(ys, *ffn1)
        u, q, c, kpe = _inproj(x1, *proj_w, cos_s, sin_s, gq, gk)
        ext = jnp.concatenate([state_conv[l], u.reshape(ns, ts, CONV_DIM)], axis=1)
        a = _conv_sample(ext, *conv_p).reshape(ms, CONV_DIM)
        c3 = c.reshape(ns, ts, KV_LORA)
        kpe3 = kpe[:, QK_NOPE:QK_HEAD].reshape(ns, ts, QK_ROPE)
        qabs = _qabs(q, wukt_pad)
        qabs = qabs.reshape(N_HEADS, ns, ts, KV_LORA).transpose(1, 2, 0, 3).reshape(ns, ts * N_HEADS, KV_LORA)
        qr = q.reshape(ns, ts, N_HEADS, HEAD_PAD)[..., QK_NOPE:QK_HEAD].reshape(ns, ts * N_HEADS, QK_ROPE)
        kpe_new_t = jnp.pad(kpe3.transpose(0, 2, 1), ((0, 0), (0, 0), (0, PAGE_SIZE - ts)))
        o = _sample_attn(page_table, qabs, qr, c3, kpe_new_t, wukt, wv, cache_ckv, cache_kpe_t, l)
        x2 = _merge(x1, row(norm_mix[l]), a, o.reshape(ms, N_HEADS * V_HEAD), *merge_w)
        ys = _ffn(x2, *ffn2)
        outs[3].append(c3)
        outs[4].append(kpe3)
        outs[5].append(ext[:, ts:, :])

    return (yp.reshape(nb, t, D_MODEL), ys.reshape(ns, ts, D_MODEL),
            jnp.stack(outs[0]), jnp.stack(outs[1]), jnp.stack(outs[2]),
            jnp.stack(outs[3]), jnp.stack(outs[4]), jnp.stack(outs[5]))
```

```python
import functools

import jax
import jax.numpy as jnp
from jax import lax
from jax.experimental import pallas as pl
from jax.experimental.pallas import tpu as pltpu

D_MODEL = 1024
CONV_DIM = 512
CONV_WIDTH = 31
N_HEADS = 8
QK_NOPE = 64
QK_ROPE = 32
QK_HEAD = QK_NOPE + QK_ROPE
V_HEAD = 64
Q_LORA = 384
KV_LORA = 256
ROPE_BASE = 10000.0
SCALE = QK_HEAD ** -0.5
LOG2E = 1.4426950408889634
D_FF = 2816
EPS = 1e-6
PAGE_SIZE = 128

SUBLANES = 8
HEAD_PAD = 128
FF_CHUNK = 256
CONV_HALO = 32
SAMPLE_PAGES_PER_CHUNK = 16
SAMPLE_SLOTS = 4
NEG_BIG = -1e30
SHIFT_SLACK = 1.01
MAX_SCORE_SHIFT = 50.0
VMEM_LIMIT = 56 * 1024 * 1024

BF16 = jnp.bfloat16
F32 = jnp.float32


def _row_tile(m, want=512):
    return want if m % want == 0 else m


def _rms(x, g):
    return x * lax.rsqrt(jnp.mean(x * x, axis=-1, keepdims=True) + EPS) * g


def _dot(a, b):
    return jnp.dot(a, b, preferred_element_type=F32)


def _dot_nt(a, b):
    return lax.dot_general(a, b, (((1,), (1,)), ((), ())), preferred_element_type=F32)


def _const_spec(shape):
    nd = len(shape)
    return pl.BlockSpec(shape, lambda *_: (0,) * nd)


def _params(sem):
    return pltpu.CompilerParams(dimension_semantics=sem, vmem_limit_bytes=VMEM_LIMIT)


def _ffn_kernel(x_ref, g_ref, wgu_ref, wd_ref, o_ref, acc_ref):
    x = x_ref[...]
    h = _rms(x, g_ref[...]).astype(BF16)
    for c in range(D_FF // FF_CHUNK):
        lo = c * FF_CHUNK
        gate = _dot(h, wgu_ref[:, lo:lo + FF_CHUNK])
        up = _dot(h, wgu_ref[:, D_FF + lo:D_FF + lo + FF_CHUNK])
        act = (gate * jax.nn.sigmoid(gate) * up).astype(BF16)
        part = _dot(act, wd_ref[lo:lo + FF_CHUNK, :])
        if c == 0:
            acc_ref[...] = part
        else:
            acc_ref[...] += part
    o_ref[...] = x + 0.5 * acc_ref[...]


def _ffn(x, g, wgu, wd):
    m = x.shape[0]
    tm = _row_tile(m)
    return pl.pallas_call(
        _ffn_kernel,
        out_shape=jax.ShapeDtypeStruct((m, D_MODEL), F32),
        grid=(m // tm,),
        in_specs=[
            pl.BlockSpec((tm, D_MODEL), lambda i: (i, 0)),
            _const_spec((1, D_MODEL)),
            _const_spec((D_MODEL, 2 * D_FF)),
            _const_spec((D_FF, D_MODEL)),
        ],
        out_specs=pl.BlockSpec((tm, D_MODEL), lambda i: (i, 0)),
        scratch_shapes=[pltpu.VMEM((tm, D_MODEL), F32)],
        compiler_params=_params(("parallel",)),
        name="ffn",
    )(x, g, wgu, wd)


_W1_QA = 2 * CONV_DIM
_W1_CKV = _W1_QA + Q_LORA
_W1_KPE = _W1_CKV + KV_LORA
_W1_KPE_SW = _W1_KPE + HEAD_PAD
_W1_COLS = _W1_KPE_SW + HEAD_PAD
_QW = N_HEADS * HEAD_PAD


def _inproj_kernel(x_ref, g_ref, w1_ref, gqa_ref, gkv_ref, wq_ref, cos_ref, sin_ref,
                   gq_ref, gk_ref, *rest, with_keys):
    if with_keys:
        wk_ref, wv_ref, aug_ref, u_ref, q_ref, c_ref, kpe_ref, k_ref, v_ref = rest
    else:
        u_ref, q_ref, c_ref, kpe_ref = rest
    h = _rms(x_ref[...], g_ref[...]).astype(BF16)
    proj = _dot(h, w1_ref[...])
    u_ref[...] = proj[:, :CONV_DIM] * jax.nn.sigmoid(proj[:, CONV_DIM:2 * CONV_DIM])

    cos = cos_ref[...]
    sin = sin_ref[...]
    qa = _rms(proj[:, _W1_QA:_W1_CKV], gqa_ref[...]).astype(BF16)
    qq = _dot(qa, wq_ref[...])
    gq = gq_ref[...]
    gk = gk_ref[...]
    for hd in range(N_HEADS):
        lo = hd * HEAD_PAD
        qh = qq[:, lo:lo + HEAD_PAD] * cos + qq[:, _QW + lo:_QW + lo + HEAD_PAD] * sin
        qh = qh * lax.rsqrt(jnp.sum(qh * qh, axis=-1, keepdims=True) * (1.0 / QK_HEAD) + EPS) * gq
        if with_keys:
            qh = qh + aug_ref[0:1, :]
        else:
            qh = qh * gk
        q_ref[:, lo:lo + HEAD_PAD] = qh.astype(BF16)

    c = _rms(proj[:, _W1_CKV:_W1_KPE], gkv_ref[...])
    c_ref[...] = c
    kpe = proj[:, _W1_KPE:_W1_KPE_SW] * cos + proj[:, _W1_KPE_SW:_W1_COLS] * sin
    kpe_ref[...] = kpe

    if with_keys:
        cb = c.astype(BF16)
        kn = _dot(cb, wk_ref[...])
        for hd in range(N_HEADS):
            lo = hd * HEAD_PAD
            kh = kn[:, lo:lo + HEAD_PAD] + kpe
            kh = kh * lax.rsqrt(jnp.sum(kh * kh, axis=-1, keepdims=True) * (1.0 / QK_HEAD) + EPS) * gk
            k_ref[:, lo:lo + HEAD_PAD] = (kh + aug_ref[1:2, :]).astype(BF16)
        v_ref[...] = (_dot(cb, wv_ref[...]) + jnp.tile(aug_ref[2:3, :], (1, N_HEADS))).astype(BF16)


def _inproj(x, g, w1, gqa, gkv, wq, cos_tab, sin_tab, gq, gk, wk=None, wv=None, aug=None):
    m = x.shape[0]
    tm = _row_tile(m)
    with_keys = wk is not None
    n_tab = cos_tab.shape[0] // tm
    row = lambda i: (i, 0)
    tab = lambda i: (i % n_tab, 0)
    in_specs = [
        pl.BlockSpec((tm, D_MODEL), row),
        _const_spec((1, D_MODEL)),
        _const_spec((D_MODEL, _W1_COLS)),
        _const_spec((1, Q_LORA)),
        _const_spec((1, KV_LORA)),
        _const_spec((Q_LORA, 2 * _QW)),
        pl.BlockSpec((tm, HEAD_PAD), tab),
        pl.BlockSpec((tm, HEAD_PAD), tab),
        _const_spec((1, HEAD_PAD)),
        _const_spec((1, HEAD_PAD)),
    ]
    args = [x, g, w1, gqa, gkv, wq, cos_tab, sin_tab, gq, gk]
    out_shape = [
        jax.ShapeDtypeStruct((m, CONV_DIM), F32),
        jax.ShapeDtypeStruct((m, _QW), BF16),
        jax.ShapeDtypeStruct((m, KV_LORA), F32),
        jax.ShapeDtypeStruct((m, HEAD_PAD), F32),
    ]
    out_specs = [
        pl.BlockSpec((tm, CONV_DIM), row),
        pl.BlockSpec((tm, _QW), row),
        pl.BlockSpec((tm, KV_LORA), row),
        pl.BlockSpec((tm, HEAD_PAD), row),
    ]
    if with_keys:
        in_specs += [_const_spec((KV_LORA, _QW)), _const_spec((KV_LORA, _QW)), _const_spec((SUBLANES, HEAD_PAD))]
        args += [wk, wv, aug]
        out_shape += [jax.ShapeDtypeStruct((m, _QW), BF16), jax.ShapeDtypeStruct((m, _QW), BF16)]
        out_specs += [pl.BlockSpec((tm, _QW), row), pl.BlockSpec((tm, _QW), row)]
    return pl.pallas_call(
        functools.partial(_inproj_kernel, with_keys=with_keys),
        out_shape=out_shape,
        grid=(m // tm,),
        in_specs=in_specs,
        out_specs=out_specs,
        compiler_params=_params(("parallel",)),
        name="inproj_keys" if with_keys else "inproj",
    )(*args)


def _conv_tail(acc, b, g, beta):
    y = acc + b
    mu = jnp.mean(y, axis=-1, keepdims=True)
    d = y - mu
    var = jnp.mean(d * d, axis=-1, keepdims=True)
    z = d * lax.rsqrt(var + EPS) * g + beta
    return (z * jax.nn.sigmoid(z)).astype(BF16)


def _conv_prompt_kernel(cur_ref, prev_ref, w_ref, b_ref, g_ref, beta_ref, o_ref, ext_ref, shift_ref):
    tt = cur_ref.shape[1]
    first = pl.program_id(1) == 0
    prev = prev_ref[0]
    ext_ref[0:CONV_HALO, :] = jnp.where(first, 0.0, prev)
    ext_ref[CONV_HALO:, :] = cur_ref[0]
    span = shift_ref.shape[1]
    for s in range(1, SUBLANES):
        shift_ref[s - 1] = ext_ref[s:s + span, :]
    base = CONV_HALO - (CONV_WIDTH - 1)
    acc = None
    for j in range(CONV_WIDTH):
        s = (base + j) % SUBLANES
        lo = base + j - s
        rows = ext_ref[lo:lo + tt, :] if s == 0 else shift_ref[s - 1, lo:lo + tt, :]
        term = w_ref[j:j + 1, :] * rows
        acc = term if acc is None else acc + term
    o_ref[0] = _conv_tail(acc, b_ref[...], g_ref[...], beta_ref[...])


def _conv_prompt(u, w, b, g, beta):
    nb, t, _ = u.shape
    tt = _row_tile(t, 256)
    per = tt // CONV_HALO
    return pl.pallas_call(
        _conv_prompt_kernel,
        out_shape=jax.ShapeDtypeStruct((nb, t, CONV_DIM), BF16),
        grid=(nb, t // tt),
        in_specs=[
            pl.BlockSpec((1, tt, CONV_DIM), lambda bi, i: (bi, i, 0)),
            pl.BlockSpec((1, CONV_HALO, CONV_DIM), lambda bi, i: (bi, jnp.maximum(i * per - 1, 0), 0)),
            _const_spec((CONV_WIDTH, CONV_DIM)),
            _const_spec((1, CONV_DIM)),
            _const_spec((1, CONV_DIM)),
            _const_spec((1, CONV_DIM)),
        ],
        out_specs=pl.BlockSpec((1, tt, CONV_DIM), lambda bi, i: (bi, i, 0)),
        scratch_shapes=[pltpu.VMEM((tt + CONV_HALO, CONV_DIM), F32),
                        pltpu.VMEM((SUBLANES - 1, tt + CONV_HALO - SUBLANES, CONV_DIM), F32)],
        compiler_params=_params(("parallel", "parallel")),
        name="conv_prompt",
    )(u, u, w, b, g, beta)


def _conv_sample_kernel(ext_ref, w_ref, b_ref, g_ref, beta_ref, o_ref):
    t = o_ref.shape[1]
    acc = w_ref[0:1, :] * ext_ref[:, 0:t, :]
    for j in range(1, CONV_WIDTH):
        acc = acc + w_ref[j:j + 1, :] * ext_ref[:, j:j + t, :]
    o_ref[...] = _conv_tail(acc, b_ref[...], g_ref[...], beta_ref[...])


def _conv_sample(ext, w, b, g, beta):
    nb, rows, _ = ext.shape
    t = rows - (CONV_WIDTH - 1)
    bs = 16 if nb % 16 == 0 else nb
    return pl.pallas_call(
        _conv_sample_kernel,
        out_shape=jax.ShapeDtypeStruct((nb, t, CONV_DIM), BF16),
        grid=(nb // bs,),
        in_specs=[
            pl.BlockSpec((bs, rows, CONV_DIM), lambda i: (i, 0, 0)),
            _const_spec((CONV_WIDTH, CONV_DIM)),
            _const_spec((1, CONV_DIM)),
            _const_spec((1, CONV_DIM)),
            _const_spec((1, CONV_DIM)),
        ],
        out_specs=pl.BlockSpec((bs, t, CONV_DIM), lambda i: (i, 0, 0)),
        compiler_params=_params(("parallel",)),
        name="conv_sample",
    )(ext, w, b, g, beta)


def _prompt_attn_kernel(q_ref, k_ref, v_ref, o_ref, *scratch, tk, bounded):
    if bounded:
        (acc_ref,) = scratch
    else:
        m_ref, acc_ref = scratch
    tq = q_ref.shape[1]
    qi = pl.program_id(1)
    row = lax.broadcasted_iota(jnp.int32, (tq, tk), 0)
    col = lax.broadcasted_iota(jnp.int32, (tq, tk), 1)
    lane = lax.broadcasted_iota(jnp.int32, (tq, HEAD_PAD), 1)

    def step(hd, j, masked):
        lo = hd * HEAD_PAD
        start = pl.multiple_of(j * tk, tk)
        kh = k_ref[0, pl.ds(start, tk), lo:lo + HEAD_PAD]
        vh = v_ref[0, pl.ds(start, tk), lo:lo + HEAD_PAD]
        s = _dot_nt(q_ref[0, :, lo:lo + HEAD_PAD], kh)
        if masked:
            s = jnp.where(col <= row, s, NEG_BIG)
        if bounded:
            acc_ref[hd] += _dot(jnp.exp2(s).astype(BF16), vh)
        else:
            m_old = m_ref[hd]
            m_new = jnp.maximum(m_old, jnp.max(s, axis=-1, keepdims=True))
            acc_ref[hd] = jnp.exp2(m_old - m_new) * acc_ref[hd] + _dot(jnp.exp2(s - m_new).astype(BF16), vh)
            m_ref[hd] = m_new

    if not bounded:
        m_ref[...] = jnp.full(m_ref.shape, NEG_BIG, F32)
    acc_ref[...] = jnp.zeros(acc_ref.shape, F32)

    def body(j, carry):
        for hd in range(N_HEADS):
            step(hd, j, False)
        return carry

    lax.fori_loop(0, qi, body, 0)
    for hd in range(N_HEADS):
        step(hd, qi, True)
    for pair in range(N_HEADS // 2):
        even = acc_ref[2 * pair]
        odd = acc_ref[2 * pair + 1]
        even = even / even[:, V_HEAD:V_HEAD + 1]
        odd = pltpu.roll(odd / odd[:, V_HEAD:V_HEAD + 1], V_HEAD, 1)
        vlo = pair * HEAD_PAD
        o_ref[0, :, vlo:vlo + HEAD_PAD] = jnp.where(lane < V_HEAD, even, odd).astype(BF16)


def _prompt_attn(q, k, v, bounded):
    nb, t, _ = q.shape
    tq = _row_tile(t)
    scratch = [pltpu.VMEM((N_HEADS, tq, HEAD_PAD), F32)]
    if not bounded:
        scratch = [pltpu.VMEM((N_HEADS, tq, 1), F32)] + scratch
    return pl.pallas_call(
        functools.partial(_prompt_attn_kernel, tk=tq, bounded=bounded),
        out_shape=jax.ShapeDtypeStruct((nb, t, N_HEADS * V_HEAD), BF16),
        grid=(nb, t // tq),
        in_specs=[
            pl.BlockSpec((1, tq, _QW), lambda bi, i: (bi, i, 0)),
            pl.BlockSpec((1, t, _QW), lambda bi, i: (bi, 0, 0)),
            pl.BlockSpec((1, t, _QW), lambda bi, i: (bi, 0, 0)),
        ],
        out_specs=pl.BlockSpec((1, tq, N_HEADS * V_HEAD), lambda bi, i: (bi, i, 0)),
        scratch_shapes=scratch,
        compiler_params=_params(("parallel", "arbitrary")),
        name="prompt_attn_bounded" if bounded else "prompt_attn",
    )(q, k, v)


def _qabs_kernel(q_ref, w_ref, o_ref):
    o_ref[0] = _dot(q_ref[...], w_ref[0]).astype(BF16)


def _qabs(q, wukt):
    m = q.shape[0]
    return pl.pallas_call(
        _qabs_kernel,
        out_shape=jax.ShapeDtypeStruct((N_HEADS, m, KV_LORA), BF16),
        grid=(N_HEADS,),
        in_specs=[
            pl.BlockSpec((m, HEAD_PAD), lambda h: (0, h)),
            pl.BlockSpec((1, HEAD_PAD, KV_LORA), lambda h: (h, 0, 0)),
        ],
        out_specs=pl.BlockSpec((1, m, KV_LORA), lambda h: (h, 0, 0)),
        compiler_params=_params(("parallel",)),
        name="qabs",
    )(q, wukt)


def _sample_attn_kernel(pt_ref, qabs_ref, qr_ref, cnew_ref, kpenew_ref, wukt_ref, wuv_ref,
                        ckv_hbm, kpe_hbm, o_ref,
                        cbuf, kbuf, sem, m_ref, l_ref, acc_ref, wq_all, s_ref, cb_ref,
                        *, layer, pages_per_chunk, n_seq):
    b = pl.program_id(0)
    n_pages = pt_ref.shape[1]
    n_chunks = n_pages // pages_per_chunk
    chunk = pages_per_chunk * PAGE_SIZE
    t_new = cnew_ref.shape[1]
    rows = N_HEADS * t_new

    n_slots = cbuf.shape[0]
    lookahead = n_slots - 1
    total = n_seq * n_chunks

    def copies(g):
        seq = g // n_chunks
        ck = g % n_chunks
        slot = g % n_slots
        out = []
        for p in range(pages_per_chunk):
            page = pt_ref[seq, ck * pages_per_chunk + p]
            dst = pl.ds(p * PAGE_SIZE, PAGE_SIZE)
            out.append(pltpu.make_async_copy(ckv_hbm.at[layer, page], cbuf.at[slot, dst], sem.at[0, slot]))
            out.append(pltpu.make_async_copy(kpe_hbm.at[layer, page], kbuf.at[slot, :, dst], sem.at[1, slot]))
        return out

    @pl.when(b == 0)
    def _():
        for g0 in range(min(lookahead, total)):
            for cp in copies(g0):
                cp.start()

    m_ref[...] = jnp.full(m_ref.shape, NEG_BIG, F32)
    l_ref[...] = jnp.zeros(l_ref.shape, F32)
    acc_ref[...] = jnp.zeros(acc_ref.shape, F32)

    n_k = N_HEADS * QK_NOPE
    wq_all[0:n_k, :] = wukt_ref[...]
    wq_all[n_k:, :] = qabs_ref[0]
    qr = qr_ref[0]

    def scores(c32, kpt32):
        n = c32.shape[0]
        cb = c32.astype(BF16)
        kt = _dot_nt(wq_all[...], cb)
        ss = jnp.sum((kt[:n_k] * kt[:n_k]).reshape(QK_NOPE, N_HEADS, n), axis=0)
        ss = ss + jnp.sum(kpt32 * kpt32, axis=0, keepdims=True)
        r = lax.rsqrt(ss * (1.0 / QK_HEAD) + EPS)
        s = kt[n_k:] + _dot(qr, kpt32.astype(BF16))
        return (s.reshape(t_new, N_HEADS, n) * r[None, :, :]).reshape(rows, n), cb

    def update(s, cb):
        m_old = m_ref[...]
        m_new = jnp.maximum(m_old, jnp.max(s, axis=-1, keepdims=True))
        alpha = jnp.exp2(m_old - m_new)
        p = jnp.exp2(s - m_new)
        l_ref[...] = alpha * l_ref[...] + jnp.sum(p, axis=-1, keepdims=True)
        acc_ref[...] = alpha * acc_ref[...] + _dot(p.astype(BF16), cb)
        m_ref[...] = m_new

    def half_step(ck, par, fold_previous=True, with_new_rows=False):
        g = b * n_chunks + ck
        slot = g % n_slots

        @pl.when(g + lookahead < total)
        def _():
            for cp in copies(g + lookahead):
                cp.start()

        width = chunk + PAGE_SIZE if with_new_rows else chunk
        if with_new_rows:
            cbuf[slot, chunk:, :] = jnp.zeros((PAGE_SIZE, KV_LORA), F32)
            cbuf[slot, chunk:chunk + t_new, :] = cnew_ref[0]
            kbuf[slot, :, chunk:] = kpenew_ref[0]
        for cp in copies(g):
            cp.wait()
        s_new, cb_new = scores(cbuf[slot, 0:width, :], kbuf[slot, :, 0:width])
        if with_new_rows:
            qpos = lax.broadcasted_iota(jnp.int32, (rows, width), 0) // N_HEADS
            kpos = lax.broadcasted_iota(jnp.int32, (rows, width), 1) - chunk
            s_new = jnp.where(kpos <= qpos, s_new, NEG_BIG)
        if fold_previous:
            update(s_ref[1 - par, :, 0:chunk], cb_ref[1 - par, 0:chunk, :])
        s_ref[par, :, 0:width] = s_new
        cb_ref[par, 0:width, :] = cb_new

    def pair_body(pr, carry):
        half_step(2 * pr + 1, 1)
        half_step(2 * pr + 2, 0)
        return carry

    half_step(0, 0, fold_previous=False)
    lax.fori_loop(0, n_chunks // 2 - 1, pair_body, 0)
    half_step(n_chunks - 1, 1, with_new_rows=True)
    update(s_ref[1], cb_ref[1])

    o_lat = (acc_ref[...] / l_ref[...]).astype(BF16)
    full = _dot(o_lat, wuv_ref[...])
    full = full.reshape(t_new, N_HEADS, N_HEADS * V_HEAD)
    own = (lax.broadcasted_iota(jnp.int32, (N_HEADS, N_HEADS * V_HEAD), 1) // V_HEAD
           == lax.broadcasted_iota(jnp.int32, (N_HEADS, N_HEADS * V_HEAD), 0))
    o_ref[0] = jnp.sum(jnp.where(own[None], full, 0.0), axis=1).astype(BF16)


def _sample_attn(page_table, qabs, qr, c_new, kpe_new, wukt, wuv, cache_ckv, cache_kpe, layer):
    nb, n_pages = page_table.shape
    t_new = c_new.shape[1]
    rows = N_HEADS * t_new
    assert n_pages % 2 == 0, "the sample kernel pipelines cache chunks in pairs"
    ppc = SAMPLE_PAGES_PER_CHUNK if n_pages % (2 * SAMPLE_PAGES_PER_CHUNK) == 0 else n_pages // 2
    chunk = ppc * PAGE_SIZE
    grid_spec = pltpu.PrefetchScalarGridSpec(
        num_scalar_prefetch=1,
        grid=(nb,),
        in_specs=[
            pl.BlockSpec((1, rows, KV_LORA), lambda b, pt: (b, 0, 0)),
            pl.BlockSpec((1, rows, QK_ROPE), lambda b, pt: (b, 0, 0)),
            pl.BlockSpec((1, t_new, KV_LORA), lambda b, pt: (b, 0, 0)),
            pl.BlockSpec((1, QK_ROPE, PAGE_SIZE), lambda b, pt: (b, 0, 0)),
            pl.BlockSpec((N_HEADS * QK_NOPE, KV_LORA), lambda b, pt: (0, 0)),
            pl.BlockSpec((KV_LORA, N_HEADS * V_HEAD), lambda b, pt: (0, 0)),
            pl.BlockSpec(memory_space=pl.ANY),
            pl.BlockSpec(memory_space=pl.ANY),
        ],
        out_specs=pl.BlockSpec((1, t_new, N_HEADS * V_HEAD), lambda b, pt: (b, 0, 0)),
        scratch_shapes=[
            pltpu.VMEM((SAMPLE_SLOTS, chunk + PAGE_SIZE, KV_LORA), F32),
            pltpu.VMEM((SAMPLE_SLOTS, QK_ROPE, chunk + PAGE_SIZE), F32),
            pltpu.SemaphoreType.DMA((2, SAMPLE_SLOTS)),
            pltpu.VMEM((rows, 1), F32),
            pltpu.VMEM((rows, 1), F32),
            pltpu.VMEM((rows, KV_LORA), F32),
            pltpu.VMEM((N_HEADS * QK_NOPE + rows, KV_LORA), BF16),
            pltpu.VMEM((2, rows, chunk + PAGE_SIZE), F32),
            pltpu.VMEM((2, chunk + PAGE_SIZE, KV_LORA), BF16),
        ],
    )
    return pl.pallas_call(
        functools.partial(_sample_attn_kernel, layer=layer, pages_per_chunk=ppc, n_seq=nb),
        out_shape=jax.ShapeDtypeStruct((nb, t_new, N_HEADS * V_HEAD), BF16),
        grid_spec=grid_spec,
        compiler_params=_params(("arbitrary",)),
        name="sample_attn",
    )(page_table, qabs, qr, c_new, kpe_new, wukt, wuv, cache_ckv, cache_kpe)


def _merge_kernel(x_ref, g_ref, a_ref, o_ref, wg_ref, wc_ref, wa_ref, wo_ref, y_ref):
    x = x_ref[...]
    h = _rms(x, g_ref[...]).astype(BF16)
    gates = jax.nn.sigmoid(_dot(h, wg_ref[...]))
    y_conv = _dot(a_ref[...], wc_ref[...])
    y_attn = _dot(o_ref[...], wa_ref[...])
    mix = gates[:, :D_MODEL] * y_conv + gates[:, D_MODEL:] * y_attn
    y_ref[...] = x + _dot(mix.astype(BF16), wo_ref[...])


def _merge(x, g, a, o, wg, wc, wa, wo):
    m = x.shape[0]
    tm = _row_tile(m)
    row = lambda i: (i, 0)
    return pl.pallas_call(
        _merge_kernel,
        out_shape=jax.ShapeDtypeStruct((m, D_MODEL), F32),
        grid=(m // tm,),
        in_specs=[
            pl.BlockSpec((tm, D_MODEL), row),
            _const_spec((1, D_MODEL)),
            pl.BlockSpec((tm, CONV_DIM), row),
            pl.BlockSpec((tm, N_HEADS * V_HEAD), row),
            _const_spec((D_MODEL, 2 * D_MODEL)),
            _const_spec((CONV_DIM, D_MODEL)),
            _const_spec((N_HEADS * V_HEAD, D_MODEL)),
            _const_spec((D_MODEL, D_MODEL)),
        ],
        out_specs=pl.BlockSpec((tm, D_MODEL), row),
        compiler_params=_params(("parallel",)),
        name="merge",
    )(x, g, a, o, wg, wc, wa, wo)


def _rotate_half_cols(w):
    half = QK_ROPE // 2
    return jnp.concatenate([-w[..., half:], w[..., :half]], axis=-1)


def _rope_tables(pos):
    inv = ROPE_BASE ** (-jnp.arange(0, QK_ROPE, 2, dtype=F32) / QK_ROPE)
    ang = pos.astype(F32)[:, None] * inv[None, :]
    cos, sin = jnp.cos(ang), jnp.sin(ang)
    n = pos.shape[0]
    pad = jnp.zeros((n, HEAD_PAD - QK_HEAD), F32)
    cos_tab = jnp.concatenate([jnp.ones((n, QK_NOPE), F32), cos, cos, pad], axis=-1)
    sin_tab = jnp.concatenate([jnp.zeros((n, QK_NOPE), F32), sin, sin, pad], axis=-1)
    return cos_tab, sin_tab


def kernel(x_prompt, x_sample, cache_ckv, cache_kpe, state_conv, page_table, norm_ffn1, ffn1_w_gu, ffn1_w_down, norm_mix, w_in, conv_w, conv_b, conv_ln_g, conv_ln_b, w_conv_out, q_a_norm, w_uq, kv_a_norm, w_uk, w_uv, q_head_norm, k_head_norm, w_attn_out, w_out, norm_ffn2, ffn2_w_gu, ffn2_w_down):
    depth = norm_ffn1.shape[0]
    nb, t, _ = x_prompt.shape
    ns, ts, _ = x_sample.shape
    past = page_table.shape[1] * PAGE_SIZE
    mp, ms = nb * t, ns * ts

    cos_p, sin_p = _rope_tables(jnp.arange(t))
    cos_s, sin_s = _rope_tables(past + jnp.arange(ts))
    tms = _row_tile(ms)
    cos_s = jnp.tile(cos_s, (tms // ts, 1))
    sin_s = jnp.tile(sin_s, (tms // ts, 1))

    cache_kpe_t = jnp.swapaxes(cache_kpe, 2, 3)
    yp = x_prompt.reshape(mp, D_MODEL)
    ys = x_sample.reshape(ms, D_MODEL)
    outs = [[] for _ in range(6)]
    for l in range(depth):
        wi = w_in[l]
        w_kpe = wi[:, _W1_KPE:_W1_KPE + QK_ROPE]
        zeros_nope = jnp.zeros((D_MODEL, QK_NOPE), F32)
        zeros_pad = jnp.zeros((D_MODEL, HEAD_PAD - QK_HEAD), F32)
        w1 = jnp.concatenate([wi[:, :_W1_KPE], zeros_nope, w_kpe, zeros_pad,
                              zeros_nope, _rotate_half_cols(w_kpe), zeros_pad], axis=-1).astype(BF16)
        w_gates = wi[:, _W1_KPE + QK_ROPE:].astype(BF16)
        wq3 = w_uq[l].reshape(Q_LORA, N_HEADS, QK_HEAD)
        q_nope, q_rope = wq3[..., :QK_NOPE], wq3[..., QK_NOPE:]
        zq_pad = jnp.zeros((Q_LORA, N_HEADS, HEAD_PAD - QK_HEAD), F32)
        zq_nope = jnp.zeros_like(q_nope)
        wq = jnp.concatenate([
            jnp.concatenate([q_nope, q_rope, zq_pad], axis=-1).reshape(Q_LORA, _QW),
            jnp.concatenate([zq_nope, _rotate_half_cols(q_rope), zq_pad], axis=-1).reshape(Q_LORA, _QW),
        ], axis=-1).astype(BF16)
        wk = jnp.pad(w_uk[l], ((0, 0), (0, 0), (0, HEAD_PAD - QK_NOPE))).reshape(KV_LORA, _QW).astype(BF16)
        wv = w_uv[l].reshape(KV_LORA, N_HEADS * V_HEAD).astype(BF16)
        wv_pad = jnp.pad(w_uv[l], ((0, 0), (0, 0), (0, HEAD_PAD - V_HEAD))).reshape(KV_LORA, _QW).astype(BF16)
        wukt = jnp.transpose(w_uk[l], (2, 1, 0)).reshape(N_HEADS * QK_NOPE, KV_LORA).astype(BF16)
        wukt_pad = jnp.pad(jnp.transpose(w_uk[l], (1, 2, 0)),
                           ((0, 0), (0, HEAD_PAD - QK_NOPE), (0, 0))).astype(BF16)
        gpad = jnp.zeros((HEAD_PAD - QK_HEAD,), F32)
        gq = jnp.concatenate([q_head_norm[l] * (SCALE * LOG2E), gpad])[None, :]
        gk = jnp.concatenate([k_head_norm[l], gpad])[None, :]
        score_bound = SHIFT_SLACK * QK_HEAD * jnp.max(jnp.abs(gq)) * jnp.max(jnp.abs(gk))
        lane_id = jnp.arange(HEAD_PAD)
        aug = jnp.zeros((SUBLANES, HEAD_PAD), F32)
        aug = aug.at[0].set(jnp.where(lane_id == QK_HEAD, -score_bound, 0.0))
        aug = aug.at[1].set(jnp.where(lane_id == QK_HEAD, 1.0, 0.0))
        aug = aug.at[2].set(jnp.where(lane_id == V_HEAD, 1.0, 0.0))
        row = lambda v: v[None, :]
        ffn1 = (row(norm_ffn1[l]), ffn1_w_gu[l].astype(BF16), ffn1_w_down[l].astype(BF16))
        ffn2 = (row(norm_ffn2[l]), ffn2_w_gu[l].astype(BF16), ffn2_w_down[l].astype(BF16))
        proj_w = (row(norm_mix[l]), w1, row(q_a_norm[l]), row(kv_a_norm[l]), wq)
        conv_p = (conv_w[l], row(conv_b[l]), row(conv_ln_g[l]), row(conv_ln_b[l]))
        merge_w = (w_gates, w_conv_out[l].astype(BF16), w_attn_out[l].astype(BF16), w_out[l].astype(BF16))

        x1 = _ffn(yp, *ffn1)
        u, q, c, kpe, k, v = _inproj(x1, *proj_w, cos_p, sin_p, gq, gk, wk, wv_pad, aug)
        u3 = u.reshape(nb, t, CONV_DIM)
        a = _conv_prompt(u3, *conv_p).reshape(mp, CONV_DIM)
        o = lax.cond(score_bound < MAX_SCORE_SHIFT,
                     functools.partial(_prompt_attn, bounded=True),
                     functools.partial(_prompt_attn, bounded=False),
                     q.reshape(nb, t, _QW), k.reshape(nb, t, _QW), v.reshape(nb, t, _QW))
        o = o.reshape(mp, N_HEADS * V_HEAD)
        x2 = _merge(x1, row(norm_mix[l]), a, o, *merge_w)
        yp = _ffn(x2, *ffn2)
        outs[0].append(c.reshape(nb, t, KV_LORA))
        outs[1].append(kpe[:, QK_NOPE:QK_HEAD].reshape(nb, t, QK_ROPE))
        outs[2].append(u3[:, t - (CONV_WIDTH - 1):, :])

        x1 = _ffn(ys, *ffn1)
        u, q, c, kpe = _inproj(x1, *proj_w, cos_s, sin_s, gq, gk)
        ext = jnp.concatenate([state_conv[l], u.reshape(ns, ts, CONV_DIM)], axis=1)
        a = _conv_sample(ext, *conv_p).reshape(ms, CONV_DIM)
        c3 = c.reshape(ns, ts, KV_LORA)
        kpe3 = kpe[:, QK_NOPE:QK_HEAD].reshape(ns, ts, QK_ROPE)
        qabs = _qabs(q, wukt_pad)
        qabs = qabs.reshape(N_HEADS, ns, ts, KV_LORA).transpose(1, 2, 0, 3).reshape(ns, ts * N_HEADS, KV_LORA)
        qr = q.reshape(ns, ts, N_HEADS, HEAD_PAD)[..., QK_NOPE:QK_HEAD].reshape(ns, ts * N_HEADS, QK_ROPE)
        kpe_new_t = jnp.pad(kpe3.transpose(0, 2, 1), ((0, 0), (0, 0), (0, PAGE_SIZE - ts)))
        o = _sample_attn(page_table, qabs, qr, c3, kpe_new_t, wukt, wv, cache_ckv, cache_kpe_t, l)
        x2 = _merge(x1, row(norm_mix[l]), a, o.reshape(ms, N_HEADS * V_HEAD), *merge_w)
        ys = _ffn(x2, *ffn2)
        outs[3].append(c3)
        outs[4].append(kpe3)
        outs[5].append(ext[:, ts:, :])

    return (yp.reshape(nb, t, D_MODEL), ys.reshape(ns, ts, D_MODEL),
            jnp.stack(outs[0]), jnp.stack(outs[1]), jnp.stack(outs[2]),
            jnp.stack(outs[3]), jnp.stack(outs[4]), jnp.stack(outs[5]))
```

```python
import functools

import jax
import jax.numpy as jnp
from jax import lax
from jax.experimental import pallas as pl
from jax.experimental.pallas import tpu as pltpu

D_MODEL = 1024
CONV_DIM = 512
CONV_WIDTH = 31
N_HEADS = 8
QK_NOPE = 64
QK_ROPE = 32
QK_HEAD = QK_NOPE + QK_ROPE
V_HEAD = 64
Q_LORA = 384
KV_LORA = 256
ROPE_BASE = 10000.0
SCALE = QK_HEAD ** -0.5
LOG2E = 1.4426950408889634
D_FF = 2816
EPS = 1e-6
PAGE_SIZE = 128

SUBLANES = 8
HEAD_PAD = 128
FF_CHUNK = 256
CONV_HALO = 32
SAMPLE_PAGES_PER_CHUNK = 16
SAMPLE_SLOTS = 4
NEG_BIG = -1e30
SHIFT_SLACK = 1.01
MAX_SCORE_SHIFT = 50.0
VMEM_LIMIT = 56 * 1024 * 1024

BF16 = jnp.bfloat16
F32 = jnp.float32


def _row_tile(m, want=512):
    return want if m % want == 0 else m


def _rms(x, g):
    return x * lax.rsqrt(jnp.mean(x * x, axis=-1, keepdims=True) + EPS) * g


def _dot(a, b):
    return jnp.dot(a, b, preferred_element_type=F32)


def _dot_nt(a, b):
    return lax.dot_general(a, b, (((1,), (1,)), ((), ())), preferred_element_type=F32)


def _const_spec(shape):
    nd = len(shape)
    return pl.BlockSpec(shape, lambda *_: (0,) * nd)


def _params(sem):
    return pltpu.CompilerParams(dimension_semantics=sem, vmem_limit_bytes=VMEM_LIMIT)


def _ffn_kernel(x_ref, g_ref, wgu_ref, wd_ref, o_ref, acc_ref):
    x = x_ref[...]
    h = _rms(x, g_ref[...]).astype(BF16)
    for c in range(D_FF // FF_CHUNK):
        lo = c * FF_CHUNK
        gate = _dot(h, wgu_ref[:, lo:lo + FF_CHUNK])
        up = _dot(h, wgu_ref[:, D_FF + lo:D_FF + lo + FF_CHUNK])
        act = (gate * jax.nn.sigmoid(gate) * up).astype(BF16)
        part = _dot(act, wd_ref[lo:lo + FF_CHUNK, :])
        if c == 0:
            acc_ref[...] = part
        else:
            acc_ref[...] += part
    o_ref[...] = x + 0.5 * acc_ref[...]


def _ffn(x, g, wgu, wd):
    m = x.shape[0]
    tm = _row_tile(m)
    return pl.pallas_call(
        _ffn_kernel,
        out_shape=jax.ShapeDtypeStruct((m, D_MODEL), F32),
        grid=(m // tm,),
        in_specs=[
            pl.BlockSpec((tm, D_MODEL), lambda i: (i, 0)),
            _const_spec((1, D_MODEL)),
            _const_spec((D_MODEL, 2 * D_FF)),
            _const_spec((D_FF, D_MODEL)),
        ],
        out_specs=pl.BlockSpec((tm, D_MODEL), lambda i: (i, 0)),
        scratch_shapes=[pltpu.VMEM((tm, D_MODEL), F32)],
        compiler_params=_params(("parallel",)),
        name="ffn",
    )(x, g, wgu, wd)


_W1_QA = 2 * CONV_DIM
_W1_CKV = _W1_QA + Q_LORA
_W1_KPE = _W1_CKV + KV_LORA
_W1_KPE_SW = _W1_KPE + HEAD_PAD
_W1_COLS = _W1_KPE_SW + HEAD_PAD
_QW = N_HEADS * HEAD_PAD


def _inproj_kernel(x_ref, g_ref, w1_ref, gqa_ref, gkv_ref, wq_ref, cos_ref, sin_ref,
                   gq_ref, gk_ref, *rest, with_keys):
    if with_keys:
        wk_ref, wv_ref, aug_ref, u_ref, q_ref, c_ref, kpe_ref, k_ref, v_ref = rest
    else:
        u_ref, q_ref, c_ref, kpe_ref = rest
    h = _rms(x_ref[...], g_ref[...]).astype(BF16)
    proj = _dot(h, w1_ref[...])
    u_ref[...] = proj[:, :CONV_DIM] * jax.nn.sigmoid(proj[:, CONV_DIM:2 * CONV_DIM])

    cos = cos_ref[...]
    sin = sin_ref[...]
    qa = _rms(proj[:, _W1_QA:_W1_CKV], gqa_ref[...]).astype(BF16)
    qq = _dot(qa, wq_ref[...])
    gq = gq_ref[...]
    gk = gk_ref[...]
    for hd in range(N_HEADS):
        lo = hd * HEAD_PAD
        qh = qq[:, lo:lo + HEAD_PAD] * cos + qq[:, _QW + lo:_QW + lo + HEAD_PAD] * sin
        qh = qh * lax.rsqrt(jnp.sum(qh * qh, axis=-1, keepdims=True) * (1.0 / QK_HEAD) + EPS) * gq
        if with_keys:
            qh = qh + aug_ref[0:1, :]
        else:
            qh = qh * gk
        q_ref[:, lo:lo + HEAD_PAD] = qh.astype(BF16)

    c = _rms(proj[:, _W1_CKV:_W1_KPE], gkv_ref[...])
    c_ref[...] = c
    kpe = proj[:, _W1_KPE:_W1_KPE_SW] * cos + proj[:, _W1_KPE_SW:_W1_COLS] * sin
    kpe_ref[...] = kpe

    if with_keys:
        cb = c.astype(BF16)
        kn = _dot(cb, wk_ref[...])
        for hd in range(N_HEADS):
            lo = hd * HEAD_PAD
            kh = kn[:, lo:lo + HEAD_PAD] + kpe
            kh = kh * lax.rsqrt(jnp.sum(kh * kh, axis=-1, keepdims=True) * (1.0 / QK_HEAD) + EPS) * gk
            k_ref[:, lo:lo + HEAD_PAD] = (kh + aug_ref[1:2, :]).astype(BF16)
        v_ref[...] = (_dot(cb, wv_ref[...]) + jnp.tile(aug_ref[2:3, :], (1, N_HEADS))).astype(BF16)


def _inproj(x, g, w1, gqa, gkv, wq, cos_tab, sin_tab, gq, gk, wk=None, wv=None, aug=None):
    m = x.shape[0]
    tm = _row_tile(m)
    with_keys = wk is not None
    n_tab = cos_tab.shape[0] // tm
    row = lambda i: (i, 0)
    tab = lambda i: (i % n_tab, 0)
    in_specs = [
        pl.BlockSpec((tm, D_MODEL), row),
        _const_spec((1, D_MODEL)),
        _const_spec((D_MODEL, _W1_COLS)),
        _const_spec((1, Q_LORA)),
        _const_spec((1, KV_LORA)),
        _const_spec((Q_LORA, 2 * _QW)),
        pl.BlockSpec((tm, HEAD_PAD), tab),
        pl.BlockSpec((tm, HEAD_PAD), tab),
        _const_spec((1, HEAD_PAD)),
        _const_spec((1, HEAD_PAD)),
    ]
    args = [x, g, w1, gqa, gkv, wq, cos_tab, sin_tab, gq, gk]
    out_shape = [
        jax.ShapeDtypeStruct((m, CONV_DIM), F32),
        jax.ShapeDtypeStruct((m, _QW), BF16),
        jax.ShapeDtypeStruct((m, KV_LORA), F32),
        jax.ShapeDtypeStruct((m, HEAD_PAD), F32),
    ]
    out_specs = [
        pl.BlockSpec((tm, CONV_DIM), row),
        pl.BlockSpec((tm, _QW), row),
        pl.BlockSpec((tm, KV_LORA), row),
        pl.BlockSpec((tm, HEAD_PAD), row),
    ]
    if with_keys:
        in_specs += [_const_spec((KV_LORA, _QW)), _const_spec((KV_LORA, _QW)), _const_spec((SUBLANES, HEAD_PAD))]
        args += [wk, wv, aug]
        out_shape += [jax.ShapeDtypeStruct((m, _QW), BF16), jax.ShapeDtypeStruct((m, _QW), BF16)]
        out_specs += [pl.BlockSpec((tm, _QW), row), pl.BlockSpec((tm, _QW), row)]
    return pl.pallas_call(
        functools.partial(_inproj_kernel, with_keys=with_keys),
        out_shape=out_shape,
        grid=(m // tm,),
        in_specs=in_specs,
        out_specs=out_specs,
        compiler_params=_params(("parallel",)),
        name="inproj_keys" if with_keys else "inproj",
    )(*args)


def _conv_tail(acc, b, g, beta):
    y = acc + b
    mu = jnp.mean(y, axis=-1, keepdims=True)
    d = y - mu
    var = jnp.mean(d * d, axis=-1, keepdims=True)
    z = d * lax.rsqrt(var + EPS) * g + beta
    return (z * jax.nn.sigmoid(z)).astype(BF16)


def _conv_prompt_kernel(cur_ref, prev_ref, w_ref, b_ref, g_ref, beta_ref, o_ref, ext_ref, shift_ref):
    tt = cur_ref.shape[1]
    first = pl.program_id(1) == 0
    prev = prev_ref[0]
    ext_ref[0:CONV_HALO, :] = jnp.where(first, 0.0, prev)
    ext_ref[CONV_HALO:, :] = cur_ref[0]
    span = shift_ref.shape[1]
    for s in range(1, SUBLANES):
        shift_ref[s - 1] = ext_ref[s:s + span, :]
    base = CONV_HALO - (CONV_WIDTH - 1)
    acc = None
    for j in range(CONV_WIDTH):
        s = (base + j) % SUBLANES
        lo = base + j - s
        rows = ext_ref[lo:lo + tt, :] if s == 0 else shift_ref[s - 1, lo:lo + tt, :]
        term = w_ref[j:j + 1, :] * rows
        acc = term if acc is None else acc + term
    o_ref[0] = _conv_tail(acc, b_ref[...], g_ref[...], beta_ref[...])


def _conv_prompt(u, w, b, g, beta):
    nb, t, _ = u.shape
    tt = _row_tile(t, 256)
    per = tt // CONV_HALO
    return pl.pallas_call(
        _conv_prompt_kernel,
        out_shape=jax.ShapeDtypeStruct((nb, t, CONV_DIM), BF16),
        grid=(nb, t // tt),
        in_specs=[
            pl.BlockSpec((1, tt, CONV_DIM), lambda bi, i: (bi, i, 0)),
            pl.BlockSpec((1, CONV_HALO, CONV_DIM), lambda bi, i: (bi, jnp.maximum(i * per - 1, 0), 0)),
            _const_spec((CONV_WIDTH, CONV_DIM)),
            _const_spec((1, CONV_DIM)),
            _const_spec((1, CONV_DIM)),
            _const_spec((1, CONV_DIM)),
        ],
        out_specs=pl.BlockSpec((1, tt, CONV_DIM), lambda bi, i: (bi, i, 0)),
        scratch_shapes=[pltpu.VMEM((tt + CONV_HALO, CONV_DIM), F32),
                        pltpu.VMEM((SUBLANES - 1, tt + CONV_HALO - SUBLANES, CONV_DIM), F32)],
        compiler_params=_params(("parallel", "parallel")),
        name="conv_prompt",
    )(u, u, w, b, g, beta)


def _conv_sample_kernel(ext_ref, w_ref, b_ref, g_ref, beta_ref, o_ref):
    t = o_ref.shape[1]
    acc = w_ref[0:1, :] * ext_ref[:, 0:t, :]
    for j in range(1, CONV_WIDTH):
        acc = acc + w_ref[j:j + 1, :] * ext_ref[:, j:j + t, :]
    o_ref[...] = _conv_tail(acc, b_ref[...], g_ref[...], beta_ref[...])


def _conv_sample(ext, w, b, g, beta):
    nb, rows, _ = ext.shape
    t = rows - (CONV_WIDTH - 1)
    bs = 16 if nb % 16 == 0 else nb
    return pl.pallas_call(
        _conv_sample_kernel,
        out_shape=jax.ShapeDtypeStruct((nb, t, CONV_DIM), BF16),
        grid=(nb // bs,),
        in_specs=[
            pl.BlockSpec((bs, rows, CONV_DIM), lambda i: (i, 0, 0)),
            _const_spec((CONV_WIDTH, CONV_DIM)),
            _const_spec((1, CONV_DIM)),
            _const_spec((1, CONV_DIM)),
            _const_spec((1, CONV_DIM)),
        ],
        out_specs=pl.BlockSpec((bs, t, CONV_DIM), lambda i: (i, 0, 0)),
        compiler_params=_params(("parallel",)),
        name="conv_sample",
    )(ext, w, b, g, beta)


def _prompt_attn_kernel(q_ref, k_ref, v_ref, o_ref, *scratch, tk, bounded):
    if bounded:
        (acc_ref,) = scratch
    else:
        m_ref, acc_ref = scratch
    tq = q_ref.shape[1]
    qi = pl.program_id(1)
    row = lax.broadcasted_iota(jnp.int32, (tq, tk), 0)
    col = lax.broadcasted_iota(jnp.int32, (tq, tk), 1)
    lane = lax.broadcasted_iota(jnp.int32, (tq, HEAD_PAD), 1)

    def step(hd, j, masked):
        lo = hd * HEAD_PAD
        start = pl.multiple_of(j * tk, tk)
        kh = k_ref[0, pl.ds(start, tk), lo:lo + HEAD_PAD]
        vh = v_ref[0, pl.ds(start, tk), lo:lo + HEAD_PAD]
        s = _dot_nt(q_ref[0, :, lo:lo + HEAD_PAD], kh)
        if masked:
            s = jnp.where(col <= row, s, NEG_BIG)
        if bounded:
            acc_ref[hd] += _dot(jnp.exp2(s).astype(BF16), vh)
        else:
            m_old = m_ref[hd]
            m_new = jnp.maximum(m_old, jnp.max(s, axis=-1, keepdims=True))
            acc_ref[hd] = jnp.exp2(m_old - m_new) * acc_ref[hd] + _dot(jnp.exp2(s - m_new).astype(BF16), vh)
            m_ref[hd] = m_new

    if not bounded:
        m_ref[...] = jnp.full(m_ref.shape, NEG_BIG, F32)
    acc_ref[...] = jnp.zeros(acc_ref.shape, F32)

    def body(j, carry):
        for hd in range(N_HEADS):
            step(hd, j, False)
        return carry

    lax.fori_loop(0, qi, body, 0)
    for hd in range(N_HEADS):
        step(hd, qi, True)
    for pair in range(N_HEADS // 2):
        even = acc_ref[2 * pair]
        odd = acc_ref[2 * pair + 1]
        even = even / even[:, V_HEAD:V_HEAD + 1]
        odd = pltpu.roll(odd / odd[:, V_HEAD:V_HEAD + 1], V_HEAD, 1)
        vlo = pair * HEAD_PAD
        o_ref[0, :, vlo:vlo + HEAD_PAD] = jnp.where(lane < V_HEAD, even, odd).astype(BF16)


def _prompt_attn(q, k, v, bounded):
    nb, t, _ = q.shape
    tq = _row_tile(t)
    scratch = [pltpu.VMEM((N_HEADS, tq, HEAD_PAD), F32)]
    if not bounded:
        scratch = [pltpu.VMEM((N_HEADS, tq, 1), F32)] + scratch
    return pl.pallas_call(
        functools.partial(_prompt_attn_kernel, tk=tq, bounded=bounded),
        out_shape=jax.ShapeDtypeStruct((nb, t, N_HEADS * V_HEAD), BF16),
        grid=(nb, t // tq),
        in_specs=[
            pl.BlockSpec((1, tq, _QW), lambda bi, i: (bi, i, 0)),
            pl.BlockSpec((1, t, _QW), lambda bi, i: (bi, 0, 0)),
            pl.BlockSpec((1, t, _QW), lambda bi, i: (bi, 0, 0)),
        ],
        out_specs=pl.BlockSpec((1, tq, N_HEADS * V_HEAD), lambda bi, i: (bi, i, 0)),
        scratch_shapes=scratch,
        compiler_params=_params(("parallel", "arbitrary")),
        name="prompt_attn_bounded" if bounded else "prompt_attn",
    )(q, k, v)


def _qabs_kernel(q_ref, w_ref, o_ref):
    o_ref[0] = _dot(q_ref[...], w_ref[0]).astype(BF16)


def _qabs(q, wukt):
    m = q.shape[0]
    return pl.pallas_call(
        _qabs_kernel,
        out_shape=jax.ShapeDtypeStruct((N_HEADS, m, KV_LORA), BF16),
        grid=(N_HEADS,),
        in_specs=[
            pl.BlockSpec((m, HEAD_PAD), lambda h: (0, h)),
            pl.BlockSpec((1, HEAD_PAD, KV_LORA), lambda h: (h, 0, 0)),
        ],
        out_specs=pl.BlockSpec((1, m, KV_LORA), lambda h: (h, 0, 0)),
        compiler_params=_params(("parallel",)),
        name="qabs",
    )(q, wukt)


def _sample_attn_kernel(pt_ref, qabs_ref, qr_ref, cnew_ref, kpenew_ref, wukt_ref, wuv_ref,
                        ckv_hbm, kpe_hbm, o_ref,
                        cbuf, kbuf, sem, m_ref, l_ref, acc_ref, wq_all, s_ref, cb_ref,
                        *, layer, pages_per_chunk, n_seq):
    b = pl.program_id(0)
    n_pages = pt_ref.shape[1]
    n_chunks = n_pages // pages_per_chunk
    chunk = pages_per_chunk * PAGE_SIZE
    t_new = cnew_ref.shape[1]
    rows = N_HEADS * t_new

    n_slots = cbuf.shape[0]
    lookahead = n_slots - 1
    total = n_seq * n_chunks

    def copies(g):
        src = jnp.minimum(g, total - 1)
        seq = src // n_chunks
        ck = src % n_chunks
        slot = g % n_slots
        out = []
        for p in range(pages_per_chunk):
            page = pt_ref[seq, ck * pages_per_chunk + p]
            dst = pl.ds(p * PAGE_SIZE, PAGE_SIZE)
            out.append(pltpu.make_async_copy(ckv_hbm.at[layer, page], cbuf.at[slot, dst], sem.at[0, slot]))
            out.append(pltpu.make_async_copy(kpe_hbm.at[layer, page], kbuf.at[slot, :, dst], sem.at[1, slot]))
        return out

    @pl.when(b == 0)
    def _():
        for g0 in range(lookahead):
            for cp in copies(g0):
                cp.start()

    m_ref[...] = jnp.full(m_ref.shape, NEG_BIG, F32)
    l_ref[...] = jnp.zeros(l_ref.shape, F32)
    acc_ref[...] = jnp.zeros(acc_ref.shape, F32)

    n_k = N_HEADS * QK_NOPE
    wq_all[0:n_k, :] = wukt_ref[...]
    wq_all[n_k:, :] = qabs_ref[0]
    qr = qr_ref[0]

    def scores(c32, kpt32):
        n = c32.shape[0]
        cb = c32.astype(BF16)
        kt = _dot_nt(wq_all[...], cb)
        ss = jnp.sum((kt[:n_k] * kt[:n_k]).reshape(QK_NOPE, N_HEADS, n), axis=0)
        ss = ss + jnp.sum(kpt32 * kpt32, axis=0, keepdims=True)
        r = lax.rsqrt(ss * (1.0 / QK_HEAD) + EPS)
        s = kt[n_k:] + _dot(qr, kpt32.astype(BF16))
        return (s.reshape(t_new, N_HEADS, n) * r[None, :, :]).reshape(rows, n), cb

    def update(s, cb):
        m_old = m_ref[...]
        m_new = jnp.maximum(m_old, jnp.max(s, axis=-1, keepdims=True))
        alpha = jnp.exp2(m_old - m_new)
        p = jnp.exp2(s - m_new)
        l_ref[...] = alpha * l_ref[...] + jnp.sum(p, axis=-1, keepdims=True)
        acc_ref[...] = alpha * acc_ref[...] + _dot(p.astype(BF16), cb)
        m_ref[...] = m_new

    def half_step(ck, par, fold_previous=True, with_new_rows=False):
        g = b * n_chunks + ck
        slot = g % n_slots

        width = chunk + PAGE_SIZE if with_new_rows else chunk
        if with_new_rows:
            cbuf[slot, chunk:, :] = jnp.zeros((PAGE_SIZE, KV_LORA), F32)
            cbuf[slot, chunk:chunk + t_new, :] = cnew_ref[0]
            kbuf[slot, :, chunk:] = kpenew_ref[0]
        for cp in copies(g):
            cp.wait()
        s_new, cb_new = scores(cbuf[slot, 0:width, :], kbuf[slot, :, 0:width])
        if with_new_rows:
            qpos = lax.broadcasted_iota(jnp.int32, (rows, width), 0) // N_HEADS
            kpos = lax.broadcasted_iota(jnp.int32, (rows, width), 1) - chunk
            s_new = jnp.where(kpos <= qpos, s_new, NEG_BIG)
        for cp in copies(g + lookahead):
            cp.start()
        if fold_previous:
            update(s_ref[1 - par, :, 0:chunk], cb_ref[1 - par, 0:chunk, :])
        s_ref[par, :, 0:width] = s_new
        cb_ref[par, 0:width, :] = cb_new

    def pair_body(pr, carry):
        half_step(2 * pr + 1, 1)
        half_step(2 * pr + 2, 0)
        return carry

    half_step(0, 0, fold_previous=False)
    lax.fori_loop(0, n_chunks // 2 - 1, pair_body, 0)
    half_step(n_chunks - 1, 1, with_new_rows=True)
    update(s_ref[1], cb_ref[1])

    o_lat = (acc_ref[...] / l_ref[...]).astype(BF16)
    full = _dot(o_lat, wuv_ref[...])
    full = full.reshape(t_new, N_HEADS, N_HEADS * V_HEAD)
    own = (lax.broadcasted_iota(jnp.int32, (N_HEADS, N_HEADS * V_HEAD), 1) // V_HEAD
           == lax.broadcasted_iota(jnp.int32, (N_HEADS, N_HEADS * V_HEAD), 0))
    o_ref[0] = jnp.sum(jnp.where(own[None], full, 0.0), axis=1).astype(BF16)

    @pl.when(b == n_seq - 1)
    def _():
        for k in range(lookahead):
            for cp in copies(total + k):
                cp.wait()


def _sample_attn(page_table, qabs, qr, c_new, kpe_new, wukt, wuv, cache_ckv, cache_kpe, layer):
    nb, n_pages = page_table.shape
    t_new = c_new.shape[1]
    rows = N_HEADS * t_new
    assert n_pages % 2 == 0, "the sample kernel pipelines cache chunks in pairs"
    ppc = SAMPLE_PAGES_PER_CHUNK if n_pages % (2 * SAMPLE_PAGES_PER_CHUNK) == 0 else n_pages // 2
    chunk = ppc * PAGE_SIZE
    grid_spec = pltpu.PrefetchScalarGridSpec(
        num_scalar_prefetch=1,
        grid=(nb,),
        in_specs=[
            pl.BlockSpec((1, rows, KV_LORA), lambda b, pt: (b, 0, 0)),
            pl.BlockSpec((1, rows, QK_ROPE), lambda b, pt: (b, 0, 0)),
            pl.BlockSpec((1, t_new, KV_LORA), lambda b, pt: (b, 0, 0)),
            pl.BlockSpec((1, QK_ROPE, PAGE_SIZE), lambda b, pt: (b, 0, 0)),
            pl.BlockSpec((N_HEADS * QK_NOPE, KV_LORA), lambda b, pt: (0, 0)),
            pl.BlockSpec((KV_LORA, N_HEADS * V_HEAD), lambda b, pt: (0, 0)),
            pl.BlockSpec(memory_space=pl.ANY),
            pl.BlockSpec(memory_space=pl.ANY),
        ],
        out_specs=pl.BlockSpec((1, t_new, N_HEADS * V_HEAD), lambda b, pt: (b, 0, 0)),
        scratch_shapes=[
            pltpu.VMEM((SAMPLE_SLOTS, chunk + PAGE_SIZE, KV_LORA), F32),
            pltpu.VMEM((SAMPLE_SLOTS, QK_ROPE, chunk + PAGE_SIZE), F32),
            pltpu.SemaphoreType.DMA((2, SAMPLE_SLOTS)),
            pltpu.VMEM((rows, 1), F32),
            pltpu.VMEM((rows, 1), F32),
            pltpu.VMEM((rows, KV_LORA), F32),
            pltpu.VMEM((N_HEADS * QK_NOPE + rows, KV_LORA), BF16),
            pltpu.VMEM((2, rows, chunk + PAGE_SIZE), F32),
            pltpu.VMEM((2, chunk + PAGE_SIZE, KV_LORA), BF16),
        ],
    )
    return pl.pallas_call(
        functools.partial(_sample_attn_kernel, layer=layer, pages_per_chunk=ppc, n_seq=nb),
        out_shape=jax.ShapeDtypeStruct((nb, t_new, N_HEADS * V_HEAD), BF16),
        grid_spec=grid_spec,
        compiler_params=_params(("arbitrary",)),
        name="sample_attn",
    )(page_table, qabs, qr, c_new, kpe_new, wukt, wuv, cache_ckv, cache_kpe)


def _merge_kernel(x_ref, g_ref, a_ref, o_ref, wg_ref, wc_ref, wa_ref, wo_ref, y_ref):
    x = x_ref[...]
    h = _rms(x, g_ref[...]).astype(BF16)
    gates = jax.nn.sigmoid(_dot(h, wg_ref[...]))
    y_conv = _dot(a_ref[...], wc_ref[...])
    y_attn = _dot(o_ref[...], wa_ref[...])
    mix = gates[:, :D_MODEL] * y_conv + gates[:, D_MODEL:] * y_attn
    y_ref[...] = x + _dot(mix.astype(BF16), wo_ref[...])


def _merge(x, g, a, o, wg, wc, wa, wo):
    m = x.shape[0]
    tm = _row_tile(m)
    row = lambda i: (i, 0)
    return pl.pallas_call(
        _merge_kernel,
        out_shape=jax.ShapeDtypeStruct((m, D_MODEL), F32),
        grid=(m // tm,),
        in_specs=[
            pl.BlockSpec((tm, D_MODEL), row),
            _const_spec((1, D_MODEL)),
            pl.BlockSpec((tm, CONV_DIM), row),
            pl.BlockSpec((tm, N_HEADS * V_HEAD), row),
            _const_spec((D_MODEL, 2 * D_MODEL)),
            _const_spec((CONV_DIM, D_MODEL)),
            _const_spec((N_HEADS * V_HEAD, D_MODEL)),
            _const_spec((D_MODEL, D_MODEL)),
        ],
        out_specs=pl.BlockSpec((tm, D_MODEL), row),
        compiler_params=_params(("parallel",)),
        name="merge",
    )(x, g, a, o, wg, wc, wa, wo)


def _rotate_half_cols(w):
    half = QK_ROPE // 2
    return jnp.concatenate([-w[..., half:], w[..., :half]], axis=-1)


def _rope_tables(pos):
    inv = ROPE_BASE ** (-jnp.arange(0, QK_ROPE, 2, dtype=F32) / QK_ROPE)
    ang = pos.astype(F32)[:, None] * inv[None, :]
    cos, sin = jnp.cos(ang), jnp.sin(ang)
    n = pos.shape[0]
    pad = jnp.zeros((n, HEAD_PAD - QK_HEAD), F32)
    cos_tab = jnp.concatenate([jnp.ones((n, QK_NOPE), F32), cos, cos, pad], axis=-1)
    sin_tab = jnp.concatenate([jnp.zeros((n, QK_NOPE), F32), sin, sin, pad], axis=-1)
    return cos_tab, sin_tab


def kernel(x_prompt, x_sample, cache_ckv, cache_kpe, state_conv, page_table, norm_ffn1, ffn1_w_gu, ffn1_w_down, norm_mix, w_in, conv_w, conv_b, conv_ln_g, conv_ln_b, w_conv_out, q_a_norm, w_uq, kv_a_norm, w_uk, w_uv, q_head_norm, k_head_norm, w_attn_out, w_out, norm_ffn2, ffn2_w_gu, ffn2_w_down):
    depth = norm_ffn1.shape[0]
    nb, t, _ = x_prompt.shape
    ns, ts, _ = x_sample.shape
    past = page_table.shape[1] * PAGE_SIZE
    mp, ms = nb * t, ns * ts

    cos_p, sin_p = _rope_tables(jnp.arange(t))
    cos_s, sin_s = _rope_tables(past + jnp.arange(ts))
    tms = _row_tile(ms)
    cos_s = jnp.tile(cos_s, (tms // ts, 1))
    sin_s = jnp.tile(sin_s, (tms // ts, 1))

    cache_kpe_t = jnp.swapaxes(cache_kpe, 2, 3)
    yp = x_prompt.reshape(mp, D_MODEL)
    ys = x_sample.reshape(ms, D_MODEL)
    outs = [[] for _ in range(6)]
    for l in range(depth):
        wi = w_in[l]
        w_kpe = wi[:, _W1_KPE:_W1_KPE + QK_ROPE]
        zeros_nope = jnp.zeros((D_MODEL, QK_NOPE), F32)
        zeros_pad = jnp.zeros((D_MODEL, HEAD_PAD - QK_HEAD), F32)
        w1 = jnp.concatenate([wi[:, :_W1_KPE], zeros_nope, w_kpe, zeros_pad,
                              zeros_nope, _rotate_half_cols(w_kpe), zeros_pad], axis=-1).astype(BF16)
        w_gates = wi[:, _W1_KPE + QK_ROPE:].astype(BF16)
        wq3 = w_uq[l].reshape(Q_LORA, N_HEADS, QK_HEAD)
        q_nope, q_rope = wq3[..., :QK_NOPE], wq3[..., QK_NOPE:]
        zq_pad = jnp.zeros((Q_LORA, N_HEADS, HEAD_PAD - QK_HEAD), F32)
        zq_nope = jnp.zeros_like(q_nope)
        wq = jnp.concatenate([
            jnp.concatenate([q_nope, q_rope, zq_pad], axis=-1).reshape(Q_LORA, _QW),
            jnp.concatenate([zq_nope, _rotate_half_cols(q_rope), zq_pad], axis=-1).reshape(Q_LORA, _QW),
        ], axis=-1).astype(BF16)
        wk = jnp.pad(w_uk[l], ((0, 0), (0, 0), (0, HEAD_PAD - QK_NOPE))).reshape(KV_LORA, _QW).astype(BF16)
        wv = w_uv[l].reshape(KV_LORA, N_HEADS * V_HEAD).astype(BF16)
        wv_pad = jnp.pad(w_uv[l], ((0, 0), (0, 0), (0, HEAD_PAD - V_HEAD))).reshape(KV_LORA, _QW).astype(BF16)
        wukt = jnp.transpose(w_uk[l], (2, 1, 0)).reshape(N_HEADS * QK_NOPE, KV_LORA).astype(BF16)
        wukt_pad = jnp.pad(jnp.transpose(w_uk[l], (1, 2, 0)),
                           ((0, 0), (0, HEAD_PAD - QK_NOPE), (0, 0))).astype(BF16)
        gpad = jnp.zeros((HEAD_PAD - QK_HEAD,), F32)
        gq = jnp.concatenate([q_head_norm[l] * (SCALE * LOG2E), gpad])[None, :]
        gk = jnp.concatenate([k_head_norm[l], gpad])[None, :]
        score_bound = SHIFT_SLACK * QK_HEAD * jnp.max(jnp.abs(gq)) * jnp.max(jnp.abs(gk))
        lane_id = jnp.arange(HEAD_PAD)
        aug = jnp.zeros((SUBLANES, HEAD_PAD), F32)
        aug = aug.at[0].set(jnp.where(lane_id == QK_HEAD, -score_bound, 0.0))
        aug = aug.at[1].set(jnp.where(lane_id == QK_HEAD, 1.0, 0.0))
        aug = aug.at[2].set(jnp.where(lane_id == V_HEAD, 1.0, 0.0))
        row = lambda v: v[None, :]
        ffn1 = (row(norm_ffn1[l]), ffn1_w_gu[l].astype(BF16), ffn1_w_down[l].astype(BF16))
        ffn2 = (row(norm_ffn2[l]), ffn2_w_gu[l].astype(BF16), ffn2_w_down[l].astype(BF16))
        proj_w = (row(norm_mix[l]), w1, row(q_a_norm[l]), row(kv_a_norm[l]), wq)
        conv_p = (conv_w[l], row(conv_b[l]), row(conv_ln_g[l]), row(conv_ln_b[l]))
        merge_w = (w_gates, w_conv_out[l].astype(BF16), w_attn_out[l].astype(BF16), w_out[l].astype(BF16))

        x1 = _ffn(yp, *ffn1)
        u, q, c, kpe, k, v = _inproj(x1, *proj_w, cos_p, sin_p, gq, gk, wk, wv_pad, aug)
        u3 = u.reshape(nb, t, CONV_DIM)
        a = _conv_prompt(u3, *conv_p).reshape(mp, CONV_DIM)
        o = lax.cond(score_bound < MAX_SCORE_SHIFT,
                     functools.partial(_prompt_attn, bounded=True),
                     functools.partial(_prompt_attn, bounded=False),
                     q.reshape(nb, t, _QW), k.reshape(nb, t, _QW), v.reshape(nb, t, _QW))
        o = o.reshape(mp, N_HEADS * V_HEAD)
        x2 = _merge(x1, row(norm_mix[l]), a, o, *merge_w)
        yp = _ffn(x2, *ffn2)
        outs[0].append(c.reshape(nb, t, KV_LORA))
        outs[1].append(kpe[:, QK_NOPE:QK_HEAD].reshape(nb, t, QK_ROPE))
        outs[2].append(u3[:, t - (CONV_WIDTH - 1):, :])

        x1 = _ffn(ys, *ffn1)
        u, q, c, kpe = _inproj(x1, *proj_w, cos_s, sin_s, gq, gk)
        ext = jnp.concatenate([state_conv[l], u.reshape(ns, ts, CONV_DIM)], axis=1)
        a = _conv_sample(ext, *conv_p).reshape(ms, CONV_DIM)
        c3 = c.reshape(ns, ts, KV_LORA)
        kpe3 = kpe[:, QK_NOPE:QK_HEAD].reshape(ns, ts, QK_ROPE)
        qabs = _qabs(q, wukt_pad)
        qabs = qabs.reshape(N_HEADS, ns, ts, KV_LORA).transpose(1, 2, 0, 3).reshape(ns, ts * N_HEADS, KV_LORA)
        qr = q.reshape(ns, ts, N_HEADS, HEAD_PAD)[..., QK_NOPE:QK_HEAD].reshape(ns, ts * N_HEADS, QK_ROPE)
        kpe_new_t = jnp.pad(kpe3.transpose(0, 2, 1), ((0, 0), (0, 0), (0, PAGE_SIZE - ts)))
        o = _sample_attn(page_table, qabs, qr, c3, kpe_new_t, wukt, wv, cache_ckv, cache_kpe_t, l)
        x2 = _merge(x1, row(norm_mix[l]), a, o.reshape(ms, N_HEADS * V_HEAD), *merge_w)
        ys = _ffn(x2, *ffn2)
        outs[3].append(c3)
        outs[4].append(kpe3)
        outs[5].append(ext[:, ts:, :])

    return (yp.reshape(nb, t, D_MODEL), ys.reshape(ns, ts, D_MODEL),
            jnp.stack(outs[0]), jnp.stack(outs[1]), jnp.stack(outs[2]),
            jnp.stack(outs[3]), jnp.stack(outs[4]), jnp.stack(outs[5]))
```

```python
import functools

import jax
import jax.numpy as jnp
from jax import lax
from jax.experimental import pallas as pl
from jax.experimental.pallas import tpu as pltpu

D_MODEL = 1024
CONV_DIM = 512
CONV_WIDTH = 31
N_HEADS = 8
QK_NOPE = 64
QK_ROPE = 32
QK_HEAD = QK_NOPE + QK_ROPE
V_HEAD = 64
Q_LORA = 384
KV_LORA = 256
ROPE_BASE = 10000.0
SCALE = QK_HEAD ** -0.5
LOG2E = 1.4426950408889634
D_FF = 2816
EPS = 1e-6
PAGE_SIZE = 128

SUBLANES = 8
HEAD_PAD = 128
FF_CHUNK = 256
CONV_HALO = 32
SAMPLE_PAGES_PER_CHUNK = 16
SAMPLE_SLOTS = 6
NEG_BIG = -1e30
SHIFT_SLACK = 1.01
MAX_SCORE_SHIFT = 50.0
VMEM_LIMIT = 56 * 1024 * 1024

BF16 = jnp.bfloat16
F32 = jnp.float32


def _row_tile(m, want=512):
    return want if m % want == 0 else m


def _rms(x, g):
    return x * lax.rsqrt(jnp.mean(x * x, axis=-1, keepdims=True) + EPS) * g


def _dot(a, b):
    return jnp.dot(a, b, preferred_element_type=F32)


def _dot_nt(a, b):
    return lax.dot_general(a, b, (((1,), (1,)), ((), ())), preferred_element_type=F32)


def _const_spec(shape):
    nd = len(shape)
    return pl.BlockSpec(shape, lambda *_: (0,) * nd)


def _params(sem):
    return pltpu.CompilerParams(dimension_semantics=sem, vmem_limit_bytes=VMEM_LIMIT)


def _ffn_kernel(x_ref, g_ref, wgu_ref, wd_ref, o_ref, acc_ref):
    x = x_ref[...]
    h = _rms(x, g_ref[...]).astype(BF16)
    for c in range(D_FF // FF_CHUNK):
        lo = c * FF_CHUNK
        gate = _dot(h, wgu_ref[:, lo:lo + FF_CHUNK])
        up = _dot(h, wgu_ref[:, D_FF + lo:D_FF + lo + FF_CHUNK])
        act = (gate * jax.nn.sigmoid(gate) * up).astype(BF16)
        part = _dot(act, wd_ref[lo:lo + FF_CHUNK, :])
        if c == 0:
            acc_ref[...] = part
        else:
            acc_ref[...] += part
    o_ref[...] = x + 0.5 * acc_ref[...]


def _ffn(x, g, wgu, wd):
    m = x.shape[0]
    tm = _row_tile(m)
    return pl.pallas_call(
        _ffn_kernel,
        out_shape=jax.ShapeDtypeStruct((m, D_MODEL), F32),
        grid=(m // tm,),
        in_specs=[
            pl.BlockSpec((tm, D_MODEL), lambda i: (i, 0)),
            _const_spec((1, D_MODEL)),
            _const_spec((D_MODEL, 2 * D_FF)),
            _const_spec((D_FF, D_MODEL)),
        ],
        out_specs=pl.BlockSpec((tm, D_MODEL), lambda i: (i, 0)),
        scratch_shapes=[pltpu.VMEM((tm, D_MODEL), F32)],
        compiler_params=_params(("parallel",)),
        name="ffn",
    )(x, g, wgu, wd)


_W1_QA = 2 * CONV_DIM
_W1_CKV = _W1_QA + Q_LORA
_W1_KPE = _W1_CKV + KV_LORA
_W1_KPE_SW = _W1_KPE + HEAD_PAD
_W1_COLS = _W1_KPE_SW + HEAD_PAD
_QW = N_HEADS * HEAD_PAD


def _inproj_kernel(x_ref, g_ref, w1_ref, gqa_ref, gkv_ref, wq_ref, cos_ref, sin_ref,
                   gq_ref, gk_ref, *rest, with_keys):
    if with_keys:
        wk_ref, wv_ref, aug_ref, u_ref, q_ref, c_ref, kpe_ref, k_ref, v_ref = rest
    else:
        u_ref, q_ref, c_ref, kpe_ref = rest
    h = _rms(x_ref[...], g_ref[...]).astype(BF16)
    proj = _dot(h, w1_ref[...])
    u_ref[...] = proj[:, :CONV_DIM] * jax.nn.sigmoid(proj[:, CONV_DIM:2 * CONV_DIM])

    cos = cos_ref[...]
    sin = sin_ref[...]
    qa = _rms(proj[:, _W1_QA:_W1_CKV], gqa_ref[...]).astype(BF16)
    qq = _dot(qa, wq_ref[...])
    gq = gq_ref[...]
    gk = gk_ref[...]
    for hd in range(N_HEADS):
        lo = hd * HEAD_PAD
        qh = qq[:, lo:lo + HEAD_PAD] * cos + qq[:, _QW + lo:_QW + lo + HEAD_PAD] * sin
        qh = qh * lax.rsqrt(jnp.sum(qh * qh, axis=-1, keepdims=True) * (1.0 / QK_HEAD) + EPS) * gq
        if with_keys:
            qh = qh + aug_ref[0:1, :]
        else:
            qh = qh * gk
        q_ref[:, lo:lo + HEAD_PAD] = qh.astype(BF16)

    c = _rms(proj[:, _W1_CKV:_W1_KPE], gkv_ref[...])
    c_ref[...] = c
    kpe = proj[:, _W1_KPE:_W1_KPE_SW] * cos + proj[:, _W1_KPE_SW:_W1_COLS] * sin
    kpe_ref[...] = kpe

    if with_keys:
        cb = c.astype(BF16)
        kn = _dot(cb, wk_ref[...])
        for hd in range(N_HEADS):
            lo = hd * HEAD_PAD
            kh = kn[:, lo:lo + HEAD_PAD] + kpe
            kh = kh * lax.rsqrt(jnp.sum(kh * kh, axis=-1, keepdims=True) * (1.0 / QK_HEAD) + EPS) * gk
            k_ref[:, lo:lo + HEAD_PAD] = (kh + aug_ref[1:2, :]).astype(BF16)
        v_ref[...] = (_dot(cb, wv_ref[...]) + jnp.tile(aug_ref[2:3, :], (1, N_HEADS))).astype(BF16)


def _inproj(x, g, w1, gqa, gkv, wq, cos_tab, sin_tab, gq, gk, wk=None, wv=None, aug=None):
    m = x.shape[0]
    tm = _row_tile(m)
    with_keys = wk is not None
    n_tab = cos_tab.shape[0] // tm
    row = lambda i: (i, 0)
    tab = lambda i: (i % n_tab, 0)
    in_specs = [
        pl.BlockSpec((tm, D_MODEL), row),
        _const_spec((1, D_MODEL)),
        _const_spec((D_MODEL, _W1_COLS)),
        _const_spec((1, Q_LORA)),
        _const_spec((1, KV_LORA)),
        _const_spec((Q_LORA, 2 * _QW)),
        pl.BlockSpec((tm, HEAD_PAD), tab),
        pl.BlockSpec((tm, HEAD_PAD), tab),
        _const_spec((1, HEAD_PAD)),
        _const_spec((1, HEAD_PAD)),
    ]
    args = [x, g, w1, gqa, gkv, wq, cos_tab, sin_tab, gq, gk]
    out_shape = [
        jax.ShapeDtypeStruct((m, CONV_DIM), F32),
        jax.ShapeDtypeStruct((m, _QW), BF16),
        jax.ShapeDtypeStruct((m, KV_LORA), F32),
        jax.ShapeDtypeStruct((m, HEAD_PAD), F32),
    ]
    out_specs = [
        pl.BlockSpec((tm, CONV_DIM), row),
        pl.BlockSpec((tm, _QW), row),
        pl.BlockSpec((tm, KV_LORA), row),
        pl.BlockSpec((tm, HEAD_PAD), row),
    ]
    if with_keys:
        in_specs += [_const_spec((KV_LORA, _QW)), _const_spec((KV_LORA, _QW)), _const_spec((SUBLANES, HEAD_PAD))]
        args += [wk, wv, aug]
        out_shape += [jax.ShapeDtypeStruct((m, _QW), BF16), jax.ShapeDtypeStruct((m, _QW), BF16)]
        out_specs += [pl.BlockSpec((tm, _QW), row), pl.BlockSpec((tm, _QW), row)]
    return pl.pallas_call(
        functools.partial(_inproj_kernel, with_keys=with_keys),
        out_shape=out_shape,
        grid=(m // tm,),
        in_specs=in_specs,
        out_specs=out_specs,
        compiler_params=_params(("parallel",)),
        name="inproj_keys" if with_keys else "inproj",
    )(*args)


def _conv_tail(acc, b, g, beta):
    y = acc + b
    mu = jnp.mean(y, axis=-1, keepdims=True)
    d = y - mu
    var = jnp.mean(d * d, axis=-1, keepdims=True)
    z = d * lax.rsqrt(var + EPS) * g + beta
    return (z * jax.nn.sigmoid(z)).astype(BF16)


def _conv_prompt_kernel(cur_ref, prev_ref, w_ref, b_ref, g_ref, beta_ref, o_ref, ext_ref, shift_ref):
    tt = cur_ref.shape[1]
    first = pl.program_id(1) == 0
    prev = prev_ref[0]
    ext_ref[0:CONV_HALO, :] = jnp.where(first, 0.0, prev)
    ext_ref[CONV_HALO:, :] = cur_ref[0]
    span = shift_ref.shape[1]
    for s in range(1, SUBLANES):
        shift_ref[s - 1] = ext_ref[s:s + span, :]
    base = CONV_HALO - (CONV_WIDTH - 1)
    acc = None
    for j in range(CONV_WIDTH):
        s = (base + j) % SUBLANES
        lo = base + j - s
        rows = ext_ref[lo:lo + tt, :] if s == 0 else shift_ref[s - 1, lo:lo + tt, :]
        term = w_ref[j:j + 1, :] * rows
        acc = term if acc is None else acc + term
    o_ref[0] = _conv_tail(acc, b_ref[...], g_ref[...], beta_ref[...])


def _conv_prompt(u, w, b, g, beta):
    nb, t, _ = u.shape
    tt = _row_tile(t, 256)
    per = tt // CONV_HALO
    return pl.pallas_call(
        _conv_prompt_kernel,
        out_shape=jax.ShapeDtypeStruct((nb, t, CONV_DIM), BF16),
        grid=(nb, t // tt),
        in_specs=[
            pl.BlockSpec((1, tt, CONV_DIM), lambda bi, i: (bi, i, 0)),
            pl.BlockSpec((1, CONV_HALO, CONV_DIM), lambda bi, i: (bi, jnp.maximum(i * per - 1, 0), 0)),
            _const_spec((CONV_WIDTH, CONV_DIM)),
            _const_spec((1, CONV_DIM)),
            _const_spec((1, CONV_DIM)),
            _const_spec((1, CONV_DIM)),
        ],
        out_specs=pl.BlockSpec((1, tt, CONV_DIM), lambda bi, i: (bi, i, 0)),
        scratch_shapes=[pltpu.VMEM((tt + CONV_HALO, CONV_DIM), F32),
                        pltpu.VMEM((SUBLANES - 1, tt + CONV_HALO - SUBLANES, CONV_DIM), F32)],
        compiler_params=_params(("parallel", "parallel")),
        name="conv_prompt",
    )(u, u, w, b, g, beta)


def _conv_sample_kernel(ext_ref, w_ref, b_ref, g_ref, beta_ref, o_ref):
    t = o_ref.shape[1]
    acc = w_ref[0:1, :] * ext_ref[:, 0:t, :]
    for j in range(1, CONV_WIDTH):
        acc = acc + w_ref[j:j + 1, :] * ext_ref[:, j:j + t, :]
    o_ref[...] = _conv_tail(acc, b_ref[...], g_ref[...], beta_ref[...])


def _conv_sample(ext, w, b, g, beta):
    nb, rows, _ = ext.shape
    t = rows - (CONV_WIDTH - 1)
    bs = 16 if nb % 16 == 0 else nb
    return pl.pallas_call(
        _conv_sample_kernel,
        out_shape=jax.ShapeDtypeStruct((nb, t, CONV_DIM), BF16),
        grid=(nb // bs,),
        in_specs=[
            pl.BlockSpec((bs, rows, CONV_DIM), lambda i: (i, 0, 0)),
            _const_spec((CONV_WIDTH, CONV_DIM)),
            _const_spec((1, CONV_DIM)),
            _const_spec((1, CONV_DIM)),
            _const_spec((1, CONV_DIM)),
        ],
        out_specs=pl.BlockSpec((bs, t, CONV_DIM), lambda i: (i, 0, 0)),
        compiler_params=_params(("parallel",)),
        name="conv_sample",
    )(ext, w, b, g, beta)


def _prompt_attn_kernel(q_ref, k_ref, v_ref, o_ref, *scratch, tk, bounded):
    if bounded:
        (acc_ref,) = scratch
    else:
        m_ref, acc_ref = scratch
    tq = q_ref.shape[1]
    qi = pl.program_id(1)
    row = lax.broadcasted_iota(jnp.int32, (tq, tk), 0)
    col = lax.broadcasted_iota(jnp.int32, (tq, tk), 1)
    lane = lax.broadcasted_iota(jnp.int32, (tq, HEAD_PAD), 1)

    def step(hd, j, masked):
        lo = hd * HEAD_PAD
        start = pl.multiple_of(j * tk, tk)
        kh = k_ref[0, pl.ds(start, tk), lo:lo + HEAD_PAD]
        vh = v_ref[0, pl.ds(start, tk), lo:lo + HEAD_PAD]
        s = _dot_nt(q_ref[0, :, lo:lo + HEAD_PAD], kh)
        if masked:
            s = jnp.where(col <= row, s, NEG_BIG)
        if bounded:
            acc_ref[hd] += _dot(jnp.exp2(s).astype(BF16), vh)
        else:
            m_old = m_ref[hd]
            m_new = jnp.maximum(m_old, jnp.max(s, axis=-1, keepdims=True))
            acc_ref[hd] = jnp.exp2(m_old - m_new) * acc_ref[hd] + _dot(jnp.exp2(s - m_new).astype(BF16), vh)
            m_ref[hd] = m_new

    if not bounded:
        m_ref[...] = jnp.full(m_ref.shape, NEG_BIG, F32)
    acc_ref[...] = jnp.zeros(acc_ref.shape, F32)

    def body(j, carry):
        for hd in range(N_HEADS):
            step(hd, j, False)
        return carry

    lax.fori_loop(0, qi, body, 0)
    for hd in range(N_HEADS):
        step(hd, qi, True)
    for pair in range(N_HEADS // 2):
        even = acc_ref[2 * pair]
        odd = acc_ref[2 * pair + 1]
        even = even / even[:, V_HEAD:V_HEAD + 1]
        odd = pltpu.roll(odd / odd[:, V_HEAD:V_HEAD + 1], V_HEAD, 1)
        vlo = pair * HEAD_PAD
        o_ref[0, :, vlo:vlo + HEAD_PAD] = jnp.where(lane < V_HEAD, even, odd).astype(BF16)


def _prompt_attn(q, k, v, bounded):
    nb, t, _ = q.shape
    tq = _row_tile(t)
    scratch = [pltpu.VMEM((N_HEADS, tq, HEAD_PAD), F32)]
    if not bounded:
        scratch = [pltpu.VMEM((N_HEADS, tq, 1), F32)] + scratch
    return pl.pallas_call(
        functools.partial(_prompt_attn_kernel, tk=tq, bounded=bounded),
        out_shape=jax.ShapeDtypeStruct((nb, t, N_HEADS * V_HEAD), BF16),
        grid=(nb, t // tq),
        in_specs=[
            pl.BlockSpec((1, tq, _QW), lambda bi, i: (bi, i, 0)),
            pl.BlockSpec((1, t, _QW), lambda bi, i: (bi, 0, 0)),
            pl.BlockSpec((1, t, _QW), lambda bi, i: (bi, 0, 0)),
        ],
        out_specs=pl.BlockSpec((1, tq, N_HEADS * V_HEAD), lambda bi, i: (bi, i, 0)),
        scratch_shapes=scratch,
        compiler_params=_params(("parallel", "arbitrary")),
        name="prompt_attn_bounded" if bounded else "prompt_attn",
    )(q, k, v)


def _qabs_kernel(q_ref, w_ref, o_ref):
    o_ref[0] = _dot(q_ref[...], w_ref[0]).astype(BF16)


def _qabs(q, wukt):
    m = q.shape[0]
    return pl.pallas_call(
        _qabs_kernel,
        out_shape=jax.ShapeDtypeStruct((N_HEADS, m, KV_LORA), BF16),
        grid=(N_HEADS,),
        in_specs=[
            pl.BlockSpec((m, HEAD_PAD), lambda h: (0, h)),
            pl.BlockSpec((1, HEAD_PAD, KV_LORA), lambda h: (h, 0, 0)),
        ],
        out_specs=pl.BlockSpec((1, m, KV_LORA), lambda h: (h, 0, 0)),
        compiler_params=_params(("parallel",)),
        name="qabs",
    )(q, wukt)


def _sample_attn_kernel(pt_ref, qabs_ref, qr_ref, cnew_ref, kpenew_ref, wukt_ref, wuv_ref,
                        ckv_hbm, kpe_hbm, o_ref,
                        cbuf, kbuf, sem, m_ref, l_ref, acc_ref, wq_all, s_ref, cb_ref,
                        *, layer, pages_per_chunk, n_seq):
    b = pl.program_id(0)
    n_pages = pt_ref.shape[1]
    n_chunks = n_pages // pages_per_chunk
    chunk = pages_per_chunk * PAGE_SIZE
    t_new = cnew_ref.shape[1]
    rows = N_HEADS * t_new

    n_slots = cbuf.shape[0]
    lookahead = n_slots - 2
    total = n_seq * n_chunks

    def copies(g):
        src = jnp.minimum(g, total - 1)
        seq = src // n_chunks
        ck = src % n_chunks
        slot = g % n_slots
        out = []
        for p in range(pages_per_chunk):
            page = pt_ref[seq, ck * pages_per_chunk + p]
            dst = pl.ds(p * PAGE_SIZE, PAGE_SIZE)
            out.append(pltpu.make_async_copy(ckv_hbm.at[layer, page], cbuf.at[slot, dst], sem.at[0, slot]))
            out.append(pltpu.make_async_copy(kpe_hbm.at[layer, page], kbuf.at[slot, :, dst], sem.at[1, slot]))
        return out

    @pl.when(b == 0)
    def _():
        for g0 in range(lookahead):
            for cp in copies(g0):
                cp.start()

    m_ref[...] = jnp.full(m_ref.shape, NEG_BIG, F32)
    l_ref[...] = jnp.zeros(l_ref.shape, F32)
    acc_ref[...] = jnp.zeros(acc_ref.shape, F32)

    n_k = N_HEADS * QK_NOPE
    wq_all[0:n_k, :] = wukt_ref[...]
    wq_all[n_k:, :] = qabs_ref[0]
    qr = qr_ref[0]

    def scores(c32, kpt32):
        n = c32.shape[0]
        cb = c32.astype(BF16)
        kt = _dot_nt(wq_all[...], cb)
        ss = jnp.sum((kt[:n_k] * kt[:n_k]).reshape(QK_NOPE, N_HEADS, n), axis=0)
        ss = ss + jnp.sum(kpt32 * kpt32, axis=0, keepdims=True)
        r = lax.rsqrt(ss * (1.0 / QK_HEAD) + EPS)
        s = kt[n_k:] + _dot(qr, kpt32.astype(BF16))
        return (s.reshape(t_new, N_HEADS, n) * r[None, :, :]).reshape(rows, n), cb

    def update(s, cb):
        m_old = m_ref[...]
        m_new = jnp.maximum(m_old, jnp.max(s, axis=-1, keepdims=True))
        alpha = jnp.exp2(m_old - m_new)
        p = jnp.exp2(s - m_new)
        l_ref[...] = alpha * l_ref[...] + jnp.sum(p, axis=-1, keepdims=True)
        acc_ref[...] = alpha * acc_ref[...] + _dot(p.astype(BF16), cb)
        m_ref[...] = m_new

    def half_step(ck, par, fold_previous=True, with_new_rows=False, waits=0):
        g = b * n_chunks + ck
        slot = g % n_slots

        width = chunk + PAGE_SIZE if with_new_rows else chunk
        if with_new_rows:
            cbuf[slot, chunk:, :] = jnp.zeros((PAGE_SIZE, KV_LORA), F32)
            cbuf[slot, chunk:chunk + t_new, :] = cnew_ref[0]
            kbuf[slot, :, chunk:] = kpenew_ref[0]
        for i in range(waits):
            for cp in copies(g + i):
                cp.wait()
        s_new, cb_new = scores(cbuf[slot, 0:width, :], kbuf[slot, :, 0:width])
        if with_new_rows:
            qpos = lax.broadcasted_iota(jnp.int32, (rows, width), 0) // N_HEADS
            kpos = lax.broadcasted_iota(jnp.int32, (rows, width), 1) - chunk
            s_new = jnp.where(kpos <= qpos, s_new, NEG_BIG)
        for cp in copies(g + lookahead):
            cp.start()
        if fold_previous:
            update(s_ref[1 - par, :, 0:chunk], cb_ref[1 - par, 0:chunk, :])
        s_ref[par, :, 0:width] = s_new
        cb_ref[par, 0:width, :] = cb_new

    def pair_body(pr, carry):
        half_step(2 * pr, 0, waits=2)
        half_step(2 * pr + 1, 1)
        return carry

    if n_chunks == 2:
        half_step(0, 0, fold_previous=False, waits=2)
    else:
        half_step(0, 0, fold_previous=False, waits=2)
        half_step(1, 1)
        lax.fori_loop(1, n_chunks // 2 - 1, pair_body, 0)
        half_step(n_chunks - 2, 0, waits=2)
    half_step(n_chunks - 1, 1, with_new_rows=True)
    update(s_ref[1], cb_ref[1])

    o_lat = (acc_ref[...] / l_ref[...]).astype(BF16)
    full = _dot(o_lat, wuv_ref[...])
    full = full.reshape(t_new, N_HEADS, N_HEADS * V_HEAD)
    own = (lax.broadcasted_iota(jnp.int32, (N_HEADS, N_HEADS * V_HEAD), 1) // V_HEAD
           == lax.broadcasted_iota(jnp.int32, (N_HEADS, N_HEADS * V_HEAD), 0))
    o_ref[0] = jnp.sum(jnp.where(own[None], full, 0.0), axis=1).astype(BF16)

    @pl.when(b == n_seq - 1)
    def _():
        for k in range(lookahead):
            for cp in copies(total + k):
                cp.wait()


def _sample_attn(page_table, qabs, qr, c_new, kpe_new, wukt, wuv, cache_ckv, cache_kpe, layer):
    nb, n_pages = page_table.shape
    t_new = c_new.shape[1]
    rows = N_HEADS * t_new
    assert n_pages % 2 == 0, "the sample kernel pipelines cache chunks in pairs"
    ppc = SAMPLE_PAGES_PER_CHUNK if n_pages % (2 * SAMPLE_PAGES_PER_CHUNK) == 0 else n_pages // 2
    chunk = ppc * PAGE_SIZE
    grid_spec = pltpu.PrefetchScalarGridSpec(
        num_scalar_prefetch=1,
        grid=(nb,),
        in_specs=[
            pl.BlockSpec((1, rows, KV_LORA), lambda b, pt: (b, 0, 0)),
            pl.BlockSpec((1, rows, QK_ROPE), lambda b, pt: (b, 0, 0)),
            pl.BlockSpec((1, t_new, KV_LORA), lambda b, pt: (b, 0, 0)),
            pl.BlockSpec((1, QK_ROPE, PAGE_SIZE), lambda b, pt: (b, 0, 0)),
            pl.BlockSpec((N_HEADS * QK_NOPE, KV_LORA), lambda b, pt: (0, 0)),
            pl.BlockSpec((KV_LORA, N_HEADS * V_HEAD), lambda b, pt: (0, 0)),
            pl.BlockSpec(memory_space=pl.ANY),
            pl.BlockSpec(memory_space=pl.ANY),
        ],
        out_specs=pl.BlockSpec((1, t_new, N_HEADS * V_HEAD), lambda b, pt: (b, 0, 0)),
        scratch_shapes=[
            pltpu.VMEM((SAMPLE_SLOTS, chunk + PAGE_SIZE, KV_LORA), F32),
            pltpu.VMEM((SAMPLE_SLOTS, QK_ROPE, chunk + PAGE_SIZE), F32),
            pltpu.SemaphoreType.DMA((2, SAMPLE_SLOTS)),
            pltpu.VMEM((rows, 1), F32),
            pltpu.VMEM((rows, 1), F32),
            pltpu.VMEM((rows, KV_LORA), F32),
            pltpu.VMEM((N_HEADS * QK_NOPE + rows, KV_LORA), BF16),
            pltpu.VMEM((2, rows, chunk + PAGE_SIZE), F32),
            pltpu.VMEM((2, chunk + PAGE_SIZE, KV_LORA), BF16),
        ],
    )
    return pl.pallas_call(
        functools.partial(_sample_attn_kernel, layer=layer, pages_per_chunk=ppc, n_seq=nb),
        out_shape=jax.ShapeDtypeStruct((nb, t_new, N_HEADS * V_HEAD), BF16),
        grid_spec=grid_spec,
        compiler_params=_params(("arbitrary",)),
        name="sample_attn",
    )(page_table, qabs, qr, c_new, kpe_new, wukt, wuv, cache_ckv, cache_kpe)


def _merge_kernel(x_ref, g_ref, a_ref, o_ref, wg_ref, wc_ref, wa_ref, wo_ref, y_ref):
    x = x_ref[...]
    h = _rms(x, g_ref[...]).astype(BF16)
    gates = jax.nn.sigmoid(_dot(h, wg_ref[...]))
    y_conv = _dot(a_ref[...], wc_ref[...])
    y_attn = _dot(o_ref[...], wa_ref[...])
    mix = gates[:, :D_MODEL] * y_conv + gates[:, D_MODEL:] * y_attn
    y_ref[...] = x + _dot(mix.astype(BF16), wo_ref[...])


def _merge(x, g, a, o, wg, wc, wa, wo):
    m = x.shape[0]
    tm = _row_tile(m)
    row = lambda i: (i, 0)
    return pl.pallas_call(
        _merge_kernel,
        out_shape=jax.ShapeDtypeStruct((m, D_MODEL), F32),
        grid=(m // tm,),
        in_specs=[
            pl.BlockSpec((tm, D_MODEL), row),
            _const_spec((1, D_MODEL)),
            pl.BlockSpec((tm, CONV_DIM), row),
            pl.BlockSpec((tm, N_HEADS * V_HEAD), row),
            _const_spec((D_MODEL, 2 * D_MODEL)),
            _const_spec((CONV_DIM, D_MODEL)),
            _const_spec((N_HEADS * V_HEAD, D_MODEL)),
            _const_spec((D_MODEL, D_MODEL)),
        ],
        out_specs=pl.BlockSpec((tm, D_MODEL), row),
        compiler_params=_params(("parallel",)),
        name="merge",
    )(x, g, a, o, wg, wc, wa, wo)


def _rotate_half_cols(w):
    half = QK_ROPE // 2
    return jnp.concatenate([-w[..., half:], w[..., :half]], axis=-1)


def _rope_tables(pos):
    inv = ROPE_BASE ** (-jnp.arange(0, QK_ROPE, 2, dtype=F32) / QK_ROPE)
    ang = pos.astype(F32)[:, None] * inv[None, :]
    cos, sin = jnp.cos(ang), jnp.sin(ang)
    n = pos.shape[0]
    pad = jnp.zeros((n, HEAD_PAD - QK_HEAD), F32)
    cos_tab = jnp.concatenate([jnp.ones((n, QK_NOPE), F32), cos, cos, pad], axis=-1)
    sin_tab = jnp.concatenate([jnp.zeros((n, QK_NOPE), F32), sin, sin, pad], axis=-1)
    return cos_tab, sin_tab


def kernel(x_prompt, x_sample, cache_ckv, cache_kpe, state_conv, page_table, norm_ffn1, ffn1_w_gu, ffn1_w_down, norm_mix, w_in, conv_w, conv_b, conv_ln_g, conv_ln_b, w_conv_out, q_a_norm, w_uq, kv_a_norm, w_uk, w_uv, q_head_norm, k_head_norm, w_attn_out, w_out, norm_ffn2, ffn2_w_gu, ffn2_w_down):
    depth = norm_ffn1.shape[0]
    nb, t, _ = x_prompt.shape
    ns, ts, _ = x_sample.shape
    past = page_table.shape[1] * PAGE_SIZE
    mp, ms = nb * t, ns * ts

    cos_p, sin_p = _rope_tables(jnp.arange(t))
    cos_s, sin_s = _rope_tables(past + jnp.arange(ts))
    tms = _row_tile(ms)
    cos_s = jnp.tile(cos_s, (tms // ts, 1))
    sin_s = jnp.tile(sin_s, (tms // ts, 1))

    cache_kpe_t = jnp.swapaxes(cache_kpe, 2, 3)
    yp = x_prompt.reshape(mp, D_MODEL)
    ys = x_sample.reshape(ms, D_MODEL)
    outs = [[] for _ in range(6)]
    for l in range(depth):
        wi = w_in[l]
        w_kpe = wi[:, _W1_KPE:_W1_KPE + QK_ROPE]
        zeros_nope = jnp.zeros((D_MODEL, QK_NOPE), F32)
        zeros_pad = jnp.zeros((D_MODEL, HEAD_PAD - QK_HEAD), F32)
        w1 = jnp.concatenate([wi[:, :_W1_KPE], zeros_nope, w_kpe, zeros_pad,
                              zeros_nope, _rotate_half_cols(w_kpe), zeros_pad], axis=-1).astype(BF16)
        w_gates = wi[:, _W1_KPE + QK_ROPE:].astype(BF16)
        wq3 = w_uq[l].reshape(Q_LORA, N_HEADS, QK_HEAD)
        q_nope, q_rope = wq3[..., :QK_NOPE], wq3[..., QK_NOPE:]
        zq_pad = jnp.zeros((Q_LORA, N_HEADS, HEAD_PAD - QK_HEAD), F32)
        zq_nope = jnp.zeros_like(q_nope)
        wq = jnp.concatenate([
            jnp.concatenate([q_nope, q_rope, zq_pad], axis=-1).reshape(Q_LORA, _QW),
            jnp.concatenate([zq_nope, _rotate_half_cols(q_rope), zq_pad], axis=-1).reshape(Q_LORA, _QW),
        ], axis=-1).astype(BF16)
        wk = jnp.pad(w_uk[l], ((0, 0), (0, 0), (0, HEAD_PAD - QK_NOPE))).reshape(KV_LORA, _QW).astype(BF16)
        wv = w_uv[l].reshape(KV_LORA, N_HEADS * V_HEAD).astype(BF16)
        wv_pad = jnp.pad(w_uv[l], ((0, 0), (0, 0), (0, HEAD_PAD - V_HEAD))).reshape(KV_LORA, _QW).astype(BF16)
        wukt = jnp.transpose(w_uk[l], (2, 1, 0)).reshape(N_HEADS * QK_NOPE, KV_LORA).astype(BF16)
        wukt_pad = jnp.pad(jnp.transpose(w_uk[l], (1, 2, 0)),
                           ((0, 0), (0, HEAD_PAD - QK_NOPE), (0, 0))).astype(BF16)
        gpad = jnp.zeros((HEAD_PAD - QK_HEAD,), F32)
        gq = jnp.concatenate([q_head_norm[l] * (SCALE * LOG2E), gpad])[None, :]
        gk = jnp.concatenate([k_head_norm[l], gpad])[None, :]
        score_bound = SHIFT_SLACK * QK_HEAD * jnp.max(jnp.abs(gq)) * jnp.max(jnp.abs(gk))
        lane_id = jnp.arange(HEAD_PAD)
        aug = jnp.zeros((SUBLANES, HEAD_PAD), F32)
        aug = aug.at[0].set(jnp.where(lane_id == QK_HEAD, -score_bound, 0.0))
        aug = aug.at[1].set(jnp.where(lane_id == QK_HEAD, 1.0, 0.0))
        aug = aug.at[2].set(jnp.where(lane_id == V_HEAD, 1.0, 0.0))
        row = lambda v: v[None, :]
        ffn1 = (row(norm_ffn1[l]), ffn1_w_gu[l].astype(BF16), ffn1_w_down[l].astype(BF16))
        ffn2 = (row(norm_ffn2[l]), ffn2_w_gu[l].astype(BF16), ffn2_w_down[l].astype(BF16))
        proj_w = (row(norm_mix[l]), w1, row(q_a_norm[l]), row(kv_a_norm[l]), wq)
        conv_p = (conv_w[l], row(conv_b[l]), row(conv_ln_g[l]), row(conv_ln_b[l]))
        merge_w = (w_gates, w_conv_out[l].astype(BF16), w_attn_out[l].astype(BF16), w_out[l].astype(BF16))

        x1 = _ffn(yp, *ffn1)
        u, q, c, kpe, k, v = _inproj(x1, *proj_w, cos_p, sin_p, gq, gk, wk, wv_pad, aug)
        u3 = u.reshape(nb, t, CONV_DIM)
        a = _conv_prompt(u3, *conv_p).reshape(mp, CONV_DIM)
        o = lax.cond(score_bound < MAX_SCORE_SHIFT,
                     functools.partial(_prompt_attn, bounded=True),
                     functools.partial(_prompt_attn, bounded=False),
                     q.reshape(nb, t, _QW), k.reshape(nb, t, _QW), v.reshape(nb, t, _QW))
        o = o.reshape(mp, N_HEADS * V_HEAD)
        x2 = _merge(x1, row(norm_mix[l]), a, o, *merge_w)
        yp = _ffn(x2, *ffn2)
        outs[0].append(c.reshape(nb, t, KV_LORA))
        outs[1].append(kpe[:, QK_NOPE:QK_HEAD].reshape(nb, t, QK_ROPE))
        outs[2].append(u3[:, t - (CONV_WIDTH - 1):, :])

        x1 = _ffn(ys, *ffn1)
        u, q, c, kpe = _inproj(x1, *proj_w, cos_s, sin_s, gq, gk)
        ext = jnp.concatenate([state_conv[l], u.reshape(ns, ts, CONV_DIM)], axis=1)
        a = _conv_sample(ext, *conv_p).reshape(ms, CONV_DIM)
        c3 = c.reshape(ns, ts, KV_LORA)
        kpe3 = kpe[:, QK_NOPE:QK_HEAD].reshape(ns, ts, QK_ROPE)
        qabs = _qabs(q, wukt_pad)
        qabs = qabs.reshape(N_HEADS, ns, ts, KV_LORA).transpose(1, 2, 0, 3).reshape(ns, ts * N_HEADS, KV_LORA)
        qr = q.reshape(ns, ts, N_HEADS, HEAD_PAD)[..., QK_NOPE:QK_HEAD].reshape(ns, ts * N_HEADS, QK_ROPE)
        kpe_new_t = jnp.pad(kpe3.transpose(0, 2, 1), ((0, 0), (0, 0), (0, PAGE_SIZE - ts)))
        o = _sample_attn(page_table, qabs, qr, c3, kpe_new_t, wukt, wv, cache_ckv, cache_kpe_t, l)
        x2 = _merge(x1, row(norm_mix[l]), a, o.reshape(ms, N_HEADS * V_HEAD), *merge_w)
        ys = _ffn(x2, *ffn2)
        outs[3].append(c3)
        outs[4].append(kpe3)
        outs[5].append(ext[:, ts:, :])

    return (yp.reshape(nb, t, D_MODEL), ys.reshape(ns, ts, D_MODEL),
            jnp.stack(outs[0]), jnp.stack(outs[1]), jnp.stack(outs[2]),
            jnp.stack(outs[3]), jnp.stack(outs[4]), jnp.stack(outs[5]))
```

```python
import functools

import jax
import jax.numpy as jnp
from jax import lax
from jax.experimental import pallas as pl
from jax.experimental.pallas import tpu as pltpu

D_MODEL = 1024
CONV_DIM = 512
CONV_WIDTH = 31
N_HEADS = 8
QK_NOPE = 64
QK_ROPE = 32
QK_HEAD = QK_NOPE + QK_ROPE
V_HEAD = 64
Q_LORA = 384
KV_LORA = 256
ROPE_BASE = 10000.0
SCALE = QK_HEAD ** -0.5
LOG2E = 1.4426950408889634
D_FF = 2816
EPS = 1e-6
PAGE_SIZE = 128

SUBLANES = 8
HEAD_PAD = 128
FF_CHUNK = 256
CONV_HALO = 32
SAMPLE_PAGES_PER_CHUNK = 16
SAMPLE_SLOTS = 8
NEG_BIG = -1e30
SHIFT_SLACK = 1.01
MAX_SCORE_SHIFT = 50.0
VMEM_LIMIT = 56 * 1024 * 1024

BF16 = jnp.bfloat16
F32 = jnp.float32


def _row_tile(m, want=512):
    return want if m % want == 0 else m


def _rms(x, g):
    return x * lax.rsqrt(jnp.mean(x * x, axis=-1, keepdims=True) + EPS) * g


def _dot(a, b):
    return jnp.dot(a, b, preferred_element_type=F32)


def _dot_nt(a, b):
    return lax.dot_general(a, b, (((1,), (1,)), ((), ())), preferred_element_type=F32)


def _const_spec(shape):
    nd = len(shape)
    return pl.BlockSpec(shape, lambda *_: (0,) * nd)


def _params(sem):
    return pltpu.CompilerParams(dimension_semantics=sem, vmem_limit_bytes=VMEM_LIMIT)


def _ffn_kernel(x_ref, g_ref, wgu_ref, wd_ref, o_ref, acc_ref):
    x = x_ref[...]
    h = _rms(x, g_ref[...]).astype(BF16)
    for c in range(D_FF // FF_CHUNK):
        lo = c * FF_CHUNK
        gate = _dot(h, wgu_ref[:, lo:lo + FF_CHUNK])
        up = _dot(h, wgu_ref[:, D_FF + lo:D_FF + lo + FF_CHUNK])
        act = (gate * jax.nn.sigmoid(gate) * up).astype(BF16)
        part = _dot(act, wd_ref[lo:lo + FF_CHUNK, :])
        if c == 0:
            acc_ref[...] = part
        else:
            acc_ref[...] += part
    o_ref[...] = x + 0.5 * acc_ref[...]


def _ffn(x, g, wgu, wd):
    m = x.shape[0]
    tm = _row_tile(m)
    return pl.pallas_call(
        _ffn_kernel,
        out_shape=jax.ShapeDtypeStruct((m, D_MODEL), F32),
        grid=(m // tm,),
        in_specs=[
            pl.BlockSpec((tm, D_MODEL), lambda i: (i, 0)),
            _const_spec((1, D_MODEL)),
            _const_spec((D_MODEL, 2 * D_FF)),
            _const_spec((D_FF, D_MODEL)),
        ],
        out_specs=pl.BlockSpec((tm, D_MODEL), lambda i: (i, 0)),
        scratch_shapes=[pltpu.VMEM((tm, D_MODEL), F32)],
        compiler_params=_params(("parallel",)),
        name="ffn",
    )(x, g, wgu, wd)


_W1_QA = 2 * CONV_DIM
_W1_CKV = _W1_QA + Q_LORA
_W1_KPE = _W1_CKV + KV_LORA
_W1_KPE_SW = _W1_KPE + HEAD_PAD
_W1_COLS = _W1_KPE_SW + HEAD_PAD
_QW = N_HEADS * HEAD_PAD


def _inproj_kernel(x_ref, g_ref, w1_ref, gqa_ref, gkv_ref, wq_ref, cos_ref, sin_ref,
                   gq_ref, gk_ref, *rest, with_keys):
    if with_keys:
        wk_ref, wv_ref, aug_ref, u_ref, q_ref, c_ref, kpe_ref, k_ref, v_ref = rest
    else:
        u_ref, q_ref, c_ref, kpe_ref = rest
    h = _rms(x_ref[...], g_ref[...]).astype(BF16)
    proj = _dot(h, w1_ref[...])
    u_ref[...] = proj[:, :CONV_DIM] * jax.nn.sigmoid(proj[:, CONV_DIM:2 * CONV_DIM])

    cos = cos_ref[...]
    sin = sin_ref[...]
    qa = _rms(proj[:, _W1_QA:_W1_CKV], gqa_ref[...]).astype(BF16)
    qq = _dot(qa, wq_ref[...])
    gq = gq_ref[...]
    gk = gk_ref[...]
    for hd in range(N_HEADS):
        lo = hd * HEAD_PAD
        qh = qq[:, lo:lo + HEAD_PAD] * cos + qq[:, _QW + lo:_QW + lo + HEAD_PAD] * sin
        qh = qh * lax.rsqrt(jnp.sum(qh * qh, axis=-1, keepdims=True) * (1.0 / QK_HEAD) + EPS) * gq
        if with_keys:
            qh = qh + aug_ref[0:1, :]
        else:
            qh = qh * gk
        q_ref[:, lo:lo + HEAD_PAD] = qh.astype(BF16)

    c = _rms(proj[:, _W1_CKV:_W1_KPE], gkv_ref[...])
    c_ref[...] = c
    kpe = proj[:, _W1_KPE:_W1_KPE_SW] * cos + proj[:, _W1_KPE_SW:_W1_COLS] * sin
    kpe_ref[...] = kpe

    if with_keys:
        cb = c.astype(BF16)
        kn = _dot(cb, wk_ref[...])
        for hd in range(N_HEADS):
            lo = hd * HEAD_PAD
            kh = kn[:, lo:lo + HEAD_PAD] + kpe
            kh = kh * lax.rsqrt(jnp.sum(kh * kh, axis=-1, keepdims=True) * (1.0 / QK_HEAD) + EPS) * gk
            k_ref[:, lo:lo + HEAD_PAD] = (kh + aug_ref[1:2, :]).astype(BF16)
        v_ref[...] = (_dot(cb, wv_ref[...]) + jnp.tile(aug_ref[2:3, :], (1, N_HEADS))).astype(BF16)


def _inproj(x, g, w1, gqa, gkv, wq, cos_tab, sin_tab, gq, gk, wk=None, wv=None, aug=None):
    m = x.shape[0]
    tm = _row_tile(m)
    with_keys = wk is not None
    n_tab = cos_tab.shape[0] // tm
    row = lambda i: (i, 0)
    tab = lambda i: (i % n_tab, 0)
    in_specs = [
        pl.BlockSpec((tm, D_MODEL), row),
        _const_spec((1, D_MODEL)),
        _const_spec((D_MODEL, _W1_COLS)),
        _const_spec((1, Q_LORA)),
        _const_spec((1, KV_LORA)),
        _const_spec((Q_LORA, 2 * _QW)),
        pl.BlockSpec((tm, HEAD_PAD), tab),
        pl.BlockSpec((tm, HEAD_PAD), tab),
        _const_spec((1, HEAD_PAD)),
        _const_spec((1, HEAD_PAD)),
    ]
    args = [x, g, w1, gqa, gkv, wq, cos_tab, sin_tab, gq, gk]
    out_shape = [
        jax.ShapeDtypeStruct((m, CONV_DIM), F32),
        jax.ShapeDtypeStruct((m, _QW), BF16),
        jax.ShapeDtypeStruct((m, KV_LORA), F32),
        jax.ShapeDtypeStruct((m, HEAD_PAD), F32),
    ]
    out_specs = [
        pl.BlockSpec((tm, CONV_DIM), row),
        pl.BlockSpec((tm, _QW), row),
        pl.BlockSpec((tm, KV_LORA), row),
        pl.BlockSpec((tm, HEAD_PAD), row),
    ]
    if with_keys:
        in_specs += [_const_spec((KV_LORA, _QW)), _const_spec((KV_LORA, _QW)), _const_spec((SUBLANES, HEAD_PAD))]
        args += [wk, wv, aug]
        out_shape += [jax.ShapeDtypeStruct((m, _QW), BF16), jax.ShapeDtypeStruct((m, _QW), BF16)]
        out_specs += [pl.BlockSpec((tm, _QW), row), pl.BlockSpec((tm, _QW), row)]
    return pl.pallas_call(
        functools.partial(_inproj_kernel, with_keys=with_keys),
        out_shape=out_shape,
        grid=(m // tm,),
        in_specs=in_specs,
        out_specs=out_specs,
        compiler_params=_params(("parallel",)),
        name="inproj_keys" if with_keys else "inproj",
    )(*args)


def _conv_tail(acc, b, g, beta):
    y = acc + b
    mu = jnp.mean(y, axis=-1, keepdims=True)
    d = y - mu
    var = jnp.mean(d * d, axis=-1, keepdims=True)
    z = d * lax.rsqrt(var + EPS) * g + beta
    return (z * jax.nn.sigmoid(z)).astype(BF16)


def _conv_prompt_kernel(cur_ref, prev_ref, w_ref, b_ref, g_ref, beta_ref, o_ref, ext_ref, shift_ref):
    tt = cur_ref.shape[1]
    first = pl.program_id(1) == 0
    prev = prev_ref[0]
    ext_ref[0:CONV_HALO, :] = jnp.where(first, 0.0, prev)
    ext_ref[CONV_HALO:, :] = cur_ref[0]
    span = shift_ref.shape[1]
    for s in range(1, SUBLANES):
        shift_ref[s - 1] = ext_ref[s:s + span, :]
    base = CONV_HALO - (CONV_WIDTH - 1)
    acc = None
    for j in range(CONV_WIDTH):
        s = (base + j) % SUBLANES
        lo = base + j - s
        rows = ext_ref[lo:lo + tt, :] if s == 0 else shift_ref[s - 1, lo:lo + tt, :]
        term = w_ref[j:j + 1, :] * rows
        acc = term if acc is None else acc + term
    o_ref[0] = _conv_tail(acc, b_ref[...], g_ref[...], beta_ref[...])


def _conv_prompt(u, w, b, g, beta):
    nb, t, _ = u.shape
    tt = _row_tile(t, 256)
    per = tt // CONV_HALO
    return pl.pallas_call(
        _conv_prompt_kernel,
        out_shape=jax.ShapeDtypeStruct((nb, t, CONV_DIM), BF16),
        grid=(nb, t // tt),
        in_specs=[
            pl.BlockSpec((1, tt, CONV_DIM), lambda bi, i: (bi, i, 0)),
            pl.BlockSpec((1, CONV_HALO, CONV_DIM), lambda bi, i: (bi, jnp.maximum(i * per - 1, 0), 0)),
            _const_spec((CONV_WIDTH, CONV_DIM)),
            _const_spec((1, CONV_DIM)),
            _const_spec((1, CONV_DIM)),
            _const_spec((1, CONV_DIM)),
        ],
        out_specs=pl.BlockSpec((1, tt, CONV_DIM), lambda bi, i: (bi, i, 0)),
        scratch_shapes=[pltpu.VMEM((tt + CONV_HALO, CONV_DIM), F32),
                        pltpu.VMEM((SUBLANES - 1, tt + CONV_HALO - SUBLANES, CONV_DIM), F32)],
        compiler_params=_params(("parallel", "parallel")),
        name="conv_prompt",
    )(u, u, w, b, g, beta)


def _conv_sample_kernel(ext_ref, w_ref, b_ref, g_ref, beta_ref, o_ref):
    t = o_ref.shape[1]
    acc = w_ref[0:1, :] * ext_ref[:, 0:t, :]
    for j in range(1, CONV_WIDTH):
        acc = acc + w_ref[j:j + 1, :] * ext_ref[:, j:j + t, :]
    o_ref[...] = _conv_tail(acc, b_ref[...], g_ref[...], beta_ref[...])


def _conv_sample(ext, w, b, g, beta):
    nb, rows, _ = ext.shape
    t = rows - (CONV_WIDTH - 1)
    bs = 16 if nb % 16 == 0 else nb
    return pl.pallas_call(
        _conv_sample_kernel,
        out_shape=jax.ShapeDtypeStruct((nb, t, CONV_DIM), BF16),
        grid=(nb // bs,),
        in_specs=[
            pl.BlockSpec((bs, rows, CONV_DIM), lambda i: (i, 0, 0)),
            _const_spec((CONV_WIDTH, CONV_DIM)),
            _const_spec((1, CONV_DIM)),
            _const_spec((1, CONV_DIM)),
            _const_spec((1, CONV_DIM)),
        ],
        out_specs=pl.BlockSpec((bs, t, CONV_DIM), lambda i: (i, 0, 0)),
        compiler_params=_params(("parallel",)),
        name="conv_sample",
    )(ext, w, b, g, beta)


def _prompt_attn_kernel(q_ref, k_ref, v_ref, o_ref, *scratch, tk, bounded):
    if bounded:
        (acc_ref,) = scratch
    else:
        m_ref, acc_ref = scratch
    tq = q_ref.shape[1]
    qi = pl.program_id(1)
    row = lax.broadcasted_iota(jnp.int32, (tq, tk), 0)
    col = lax.broadcasted_iota(jnp.int32, (tq, tk), 1)
    lane = lax.broadcasted_iota(jnp.int32, (tq, HEAD_PAD), 1)

    def step(hd, j, masked):
        lo = hd * HEAD_PAD
        start = pl.multiple_of(j * tk, tk)
        kh = k_ref[0, pl.ds(start, tk), lo:lo + HEAD_PAD]
        vh = v_ref[0, pl.ds(start, tk), lo:lo + HEAD_PAD]
        s = _dot_nt(q_ref[0, :, lo:lo + HEAD_PAD], kh)
        if masked:
            s = jnp.where(col <= row, s, NEG_BIG)
        if bounded:
            acc_ref[hd] += _dot(jnp.exp2(s).astype(BF16), vh)
        else:
            m_old = m_ref[hd]
            m_new = jnp.maximum(m_old, jnp.max(s, axis=-1, keepdims=True))
            acc_ref[hd] = jnp.exp2(m_old - m_new) * acc_ref[hd] + _dot(jnp.exp2(s - m_new).astype(BF16), vh)
            m_ref[hd] = m_new

    if not bounded:
        m_ref[...] = jnp.full(m_ref.shape, NEG_BIG, F32)
    acc_ref[...] = jnp.zeros(acc_ref.shape, F32)

    def body(j, carry):
        for hd in range(N_HEADS):
            step(hd, j, False)
        return carry

    lax.fori_loop(0, qi, body, 0)
    for hd in range(N_HEADS):
        step(hd, qi, True)
    for pair in range(N_HEADS // 2):
        even = acc_ref[2 * pair]
        odd = acc_ref[2 * pair + 1]
        even = even / even[:, V_HEAD:V_HEAD + 1]
        odd = pltpu.roll(odd / odd[:, V_HEAD:V_HEAD + 1], V_HEAD, 1)
        vlo = pair * HEAD_PAD
        o_ref[0, :, vlo:vlo + HEAD_PAD] = jnp.where(lane < V_HEAD, even, odd).astype(BF16)


def _prompt_attn(q, k, v, bounded):
    nb, t, _ = q.shape
    tq = _row_tile(t)
    scratch = [pltpu.VMEM((N_HEADS, tq, HEAD_PAD), F32)]
    if not bounded:
        scratch = [pltpu.VMEM((N_HEADS, tq, 1), F32)] + scratch
    return pl.pallas_call(
        functools.partial(_prompt_attn_kernel, tk=tq, bounded=bounded),
        out_shape=jax.ShapeDtypeStruct((nb, t, N_HEADS * V_HEAD), BF16),
        grid=(nb, t // tq),
        in_specs=[
            pl.BlockSpec((1, tq, _QW), lambda bi, i: (bi, i, 0)),
            pl.BlockSpec((1, t, _QW), lambda bi, i: (bi, 0, 0)),
            pl.BlockSpec((1, t, _QW), lambda bi, i: (bi, 0, 0)),
        ],
        out_specs=pl.BlockSpec((1, tq, N_HEADS * V_HEAD), lambda bi, i: (bi, i, 0)),
        scratch_shapes=scratch,
        compiler_params=_params(("parallel", "arbitrary")),
        name="prompt_attn_bounded" if bounded else "prompt_attn",
    )(q, k, v)


def _qabs_kernel(q_ref, w_ref, o_ref):
    o_ref[0] = _dot(q_ref[...], w_ref[0]).astype(BF16)


def _qabs(q, wukt):
    m = q.shape[0]
    return pl.pallas_call(
        _qabs_kernel,
        out_shape=jax.ShapeDtypeStruct((N_HEADS, m, KV_LORA), BF16),
        grid=(N_HEADS,),
        in_specs=[
            pl.BlockSpec((m, HEAD_PAD), lambda h: (0, h)),
            pl.BlockSpec((1, HEAD_PAD, KV_LORA), lambda h: (h, 0, 0)),
        ],
        out_specs=pl.BlockSpec((1, m, KV_LORA), lambda h: (h, 0, 0)),
        compiler_params=_params(("parallel",)),
        name="qabs",
    )(q, wukt)


def _sample_attn_kernel(pt_ref, qabs_ref, qr_ref, cnew_ref, kpenew_ref, wukt_ref, wuv_ref,
                        ckv_hbm, kpe_hbm, o_ref,
                        cbuf, kbuf, sem, m_ref, l_ref, acc_ref, wq_all, s_ref, cb_ref,
                        *, layer, pages_per_chunk, n_seq):
    b = pl.program_id(0)
    n_pages = pt_ref.shape[1]
    n_chunks = n_pages // pages_per_chunk
    chunk = pages_per_chunk * PAGE_SIZE
    t_new = cnew_ref.shape[1]
    rows = N_HEADS * t_new

    n_slots = cbuf.shape[0]
    group = 4 if n_chunks % 4 == 0 else 2
    lookahead = n_slots - group
    total = n_seq * n_chunks

    def copies(g):
        src = jnp.minimum(g, total - 1)
        seq = src // n_chunks
        ck = src % n_chunks
        slot = g % n_slots
        out = []
        for p in range(pages_per_chunk):
            page = pt_ref[seq, ck * pages_per_chunk + p]
            dst = pl.ds(p * PAGE_SIZE, PAGE_SIZE)
            out.append(pltpu.make_async_copy(ckv_hbm.at[layer, page], cbuf.at[slot, dst], sem.at[0, slot]))
            out.append(pltpu.make_async_copy(kpe_hbm.at[layer, page], kbuf.at[slot, :, dst], sem.at[1, slot]))
        return out

    @pl.when(b == 0)
    def _():
        for g0 in range(lookahead):
            for cp in copies(g0):
                cp.start()

    m_ref[...] = jnp.full(m_ref.shape, NEG_BIG, F32)
    l_ref[...] = jnp.zeros(l_ref.shape, F32)
    acc_ref[...] = jnp.zeros(acc_ref.shape, F32)

    n_k = N_HEADS * QK_NOPE
    wq_all[0:n_k, :] = wukt_ref[...]
    wq_all[n_k:, :] = qabs_ref[0]
    qr = qr_ref[0]

    def scores(c32, kpt32):
        n = c32.shape[0]
        cb = c32.astype(BF16)
        kt = _dot_nt(wq_all[...], cb)
        ss = jnp.sum((kt[:n_k] * kt[:n_k]).reshape(QK_NOPE, N_HEADS, n), axis=0)
        ss = ss + jnp.sum(kpt32 * kpt32, axis=0, keepdims=True)
        r = lax.rsqrt(ss * (1.0 / QK_HEAD) + EPS)
        s = kt[n_k:] + _dot(qr, kpt32.astype(BF16))
        return (s.reshape(t_new, N_HEADS, n) * r[None, :, :]).reshape(rows, n), cb

    def update(s, cb):
        m_old = m_ref[...]
        m_new = jnp.maximum(m_old, jnp.max(s, axis=-1, keepdims=True))
        alpha = jnp.exp2(m_old - m_new)
        p = jnp.exp2(s - m_new)
        l_ref[...] = alpha * l_ref[...] + jnp.sum(p, axis=-1, keepdims=True)
        acc_ref[...] = alpha * acc_ref[...] + _dot(p.astype(BF16), cb)
        m_ref[...] = m_new

    def half_step(ck, par, fold_previous=True, with_new_rows=False, waits=0):
        g = b * n_chunks + ck
        slot = g % n_slots

        width = chunk + PAGE_SIZE if with_new_rows else chunk
        if with_new_rows:
            cbuf[slot, chunk:, :] = jnp.zeros((PAGE_SIZE, KV_LORA), F32)
            cbuf[slot, chunk:chunk + t_new, :] = cnew_ref[0]
            kbuf[slot, :, chunk:] = kpenew_ref[0]
        for i in range(waits):
            for cp in copies(g + i):
                cp.wait()
        s_new, cb_new = scores(cbuf[slot, 0:width, :], kbuf[slot, :, 0:width])
        if with_new_rows:
            qpos = lax.broadcasted_iota(jnp.int32, (rows, width), 0) // N_HEADS
            kpos = lax.broadcasted_iota(jnp.int32, (rows, width), 1) - chunk
            s_new = jnp.where(kpos <= qpos, s_new, NEG_BIG)
        for cp in copies(g + lookahead):
            cp.start()
        if fold_previous:
            update(s_ref[1 - par, :, 0:chunk], cb_ref[1 - par, 0:chunk, :])
        s_ref[par, :, 0:width] = s_new
        cb_ref[par, 0:width, :] = cb_new

    def group_steps(ck, first=False, last=False):
        for i in range(group):
            half_step(ck + i, i % 2, fold_previous=not (first and i == 0),
                      with_new_rows=last and i == group - 1, waits=group if i == 0 else 0)

    def group_body(gr, carry):
        group_steps(group * gr)
        return carry

    n_groups = n_chunks // group
    if n_groups == 1:
        group_steps(0, first=True, last=True)
    else:
        group_steps(0, first=True)
        lax.fori_loop(1, n_groups - 1, group_body, 0)
        group_steps(n_chunks - group, last=True)
    update(s_ref[1], cb_ref[1])

    o_lat = (acc_ref[...] / l_ref[...]).astype(BF16)
    full = _dot(o_lat, wuv_ref[...])
    full = full.reshape(t_new, N_HEADS, N_HEADS * V_HEAD)
    own = (lax.broadcasted_iota(jnp.int32, (N_HEADS, N_HEADS * V_HEAD), 1) // V_HEAD
           == lax.broadcasted_iota(jnp.int32, (N_HEADS, N_HEADS * V_HEAD), 0))
    o_ref[0] = jnp.sum(jnp.where(own[None], full, 0.0), axis=1).astype(BF16)

    @pl.when(b == n_seq - 1)
    def _():
        for k in range(lookahead):
            for cp in copies(total + k):
                cp.wait()


def _sample_attn(page_table, qabs, qr, c_new, kpe_new, wukt, wuv, cache_ckv, cache_kpe, layer):
    nb, n_pages = page_table.shape
    t_new = c_new.shape[1]
    rows = N_HEADS * t_new
    assert n_pages % 2 == 0, "the sample kernel pipelines cache chunks in pairs"
    ppc = SAMPLE_PAGES_PER_CHUNK if n_pages % (2 * SAMPLE_PAGES_PER_CHUNK) == 0 else n_pages // 2
    chunk = ppc * PAGE_SIZE
    grid_spec = pltpu.PrefetchScalarGridSpec(
        num_scalar_prefetch=1,
        grid=(nb,),
        in_specs=[
            pl.BlockSpec((1, rows, KV_LORA), lambda b, pt: (b, 0, 0)),
            pl.BlockSpec((1, rows, QK_ROPE), lambda b, pt: (b, 0, 0)),
            pl.BlockSpec((1, t_new, KV_LORA), lambda b, pt: (b, 0, 0)),
            pl.BlockSpec((1, QK_ROPE, PAGE_SIZE), lambda b, pt: (b, 0, 0)),
            pl.BlockSpec((N_HEADS * QK_NOPE, KV_LORA), lambda b, pt: (0, 0)),
            pl.BlockSpec((KV_LORA, N_HEADS * V_HEAD), lambda b, pt: (0, 0)),
            pl.BlockSpec(memory_space=pl.ANY),
            pl.BlockSpec(memory_space=pl.ANY),
        ],
        out_specs=pl.BlockSpec((1, t_new, N_HEADS * V_HEAD), lambda b, pt: (b, 0, 0)),
        scratch_shapes=[
            pltpu.VMEM((SAMPLE_SLOTS, chunk + PAGE_SIZE, KV_LORA), F32),
            pltpu.VMEM((SAMPLE_SLOTS, QK_ROPE, chunk + PAGE_SIZE), F32),
            pltpu.SemaphoreType.DMA((2, SAMPLE_SLOTS)),
            pltpu.VMEM((rows, 1), F32),
            pltpu.VMEM((rows, 1), F32),
            pltpu.VMEM((rows, KV_LORA), F32),
            pltpu.VMEM((N_HEADS * QK_NOPE + rows, KV_LORA), BF16),
            pltpu.VMEM((2, rows, chunk + PAGE_SIZE), F32),
            pltpu.VMEM((2, chunk + PAGE_SIZE, KV_LORA), BF16),
        ],
    )
    return pl.pallas_call(
        functools.partial(_sample_attn_kernel, layer=layer, pages_per_chunk=ppc, n_seq=nb),
        out_shape=jax.ShapeDtypeStruct((nb, t_new, N_HEADS * V_HEAD), BF16),
        grid_spec=grid_spec,
        compiler_params=_params(("arbitrary",)),
        name="sample_attn",
    )(page_table, qabs, qr, c_new, kpe_new, wukt, wuv, cache_ckv, cache_kpe)


def _merge_kernel(x_ref, g_ref, a_ref, o_ref, wg_ref, wc_ref, wa_ref, wo_ref, y_ref):
    x = x_ref[...]
    h = _rms(x, g_ref[...]).astype(BF16)
    gates = jax.nn.sigmoid(_dot(h, wg_ref[...]))
    y_conv = _dot(a_ref[...], wc_ref[...])
    y_attn = _dot(o_ref[...], wa_ref[...])
    mix = gates[:, :D_MODEL] * y_conv + gates[:, D_MODEL:] * y_attn
    y_ref[...] = x + _dot(mix.astype(BF16), wo_ref[...])


def _merge(x, g, a, o, wg, wc, wa, wo):
    m = x.shape[0]
    tm = _row_tile(m)
    row = lambda i: (i, 0)
    return pl.pallas_call(
        _merge_kernel,
        out_shape=jax.ShapeDtypeStruct((m, D_MODEL), F32),
        grid=(m // tm,),
        in_specs=[
            pl.BlockSpec((tm, D_MODEL), row),
            _const_spec((1, D_MODEL)),
            pl.BlockSpec((tm, CONV_DIM), row),
            pl.BlockSpec((tm, N_HEADS * V_HEAD), row),
            _const_spec((D_MODEL, 2 * D_MODEL)),
            _const_spec((CONV_DIM, D_MODEL)),
            _const_spec((N_HEADS * V_HEAD, D_MODEL)),
            _const_spec((D_MODEL, D_MODEL)),
        ],
        out_specs=pl.BlockSpec((tm, D_MODEL), row),
        compiler_params=_params(("parallel",)),
        name="merge",
    )(x, g, a, o, wg, wc, wa, wo)


def _rotate_half_cols(w):
    half = QK_ROPE // 2
    return jnp.concatenate([-w[..., half:], w[..., :half]], axis=-1)


def _rope_tables(pos):
    inv = ROPE_BASE ** (-jnp.arange(0, QK_ROPE, 2, dtype=F32) / QK_ROPE)
    ang = pos.astype(F32)[:, None] * inv[None, :]
    cos, sin = jnp.cos(ang), jnp.sin(ang)
    n = pos.shape[0]
    pad = jnp.zeros((n, HEAD_PAD - QK_HEAD), F32)
    cos_tab = jnp.concatenate([jnp.ones((n, QK_NOPE), F32), cos, cos, pad], axis=-1)
    sin_tab = jnp.concatenate([jnp.zeros((n, QK_NOPE), F32), sin, sin, pad], axis=-1)
    return cos_tab, sin_tab


def kernel(x_prompt, x_sample, cache_ckv, cache_kpe, state_conv, page_table, norm_ffn1, ffn1_w_gu, ffn1_w_down, norm_mix, w_in, conv_w, conv_b, conv_ln_g, conv_ln_b, w_conv_out, q_a_norm, w_uq, kv_a_norm, w_uk, w_uv, q_head_norm, k_head_norm, w_attn_out, w_out, norm_ffn2, ffn2_w_gu, ffn2_w_down):
    depth = norm_ffn1.shape[0]
    nb, t, _ = x_prompt.shape
    ns, ts, _ = x_sample.shape
    past = page_table.shape[1] * PAGE_SIZE
    mp, ms = nb * t, ns * ts

    cos_p, sin_p = _rope_tables(jnp.arange(t))
    cos_s, sin_s = _rope_tables(past + jnp.arange(ts))
    tms = _row_tile(ms)
    cos_s = jnp.tile(cos_s, (tms // ts, 1))
    sin_s = jnp.tile(sin_s, (tms // ts, 1))

    cache_kpe_t = jnp.swapaxes(cache_kpe, 2, 3)
    yp = x_prompt.reshape(mp, D_MODEL)
    ys = x_sample.reshape(ms, D_MODEL)
    outs = [[] for _ in range(6)]
    for l in range(depth):
        wi = w_in[l]
        w_kpe = wi[:, _W1_KPE:_W1_KPE + QK_ROPE]
        zeros_nope = jnp.zeros((D_MODEL, QK_NOPE), F32)
        zeros_pad = jnp.zeros((D_MODEL, HEAD_PAD - QK_HEAD), F32)
        w1 = jnp.concatenate([wi[:, :_W1_KPE], zeros_nope, w_kpe, zeros_pad,
                              zeros_nope, _rotate_half_cols(w_kpe), zeros_pad], axis=-1).astype(BF16)
        w_gates = wi[:, _W1_KPE + QK_ROPE:].astype(BF16)
        wq3 = w_uq[l].reshape(Q_LORA, N_HEADS, QK_HEAD)
        q_nope, q_rope = wq3[..., :QK_NOPE], wq3[..., QK_NOPE:]
        zq_pad = jnp.zeros((Q_LORA, N_HEADS, HEAD_PAD - QK_HEAD), F32)
        zq_nope = jnp.zeros_like(q_nope)
        wq = jnp.concatenate([
            jnp.concatenate([q_nope, q_rope, zq_pad], axis=-1).reshape(Q_LORA, _QW),
            jnp.concatenate([zq_nope, _rotate_half_cols(q_rope), zq_pad], axis=-1).reshape(Q_LORA, _QW),
        ], axis=-1).astype(BF16)
        wk = jnp.pad(w_uk[l], ((0, 0), (0, 0), (0, HEAD_PAD - QK_NOPE))).reshape(KV_LORA, _QW).astype(BF16)
        wv = w_uv[l].reshape(KV_LORA, N_HEADS * V_HEAD).astype(BF16)
        wv_pad = jnp.pad(w_uv[l], ((0, 0), (0, 0), (0, HEAD_PAD - V_HEAD))).reshape(KV_LORA, _QW).astype(BF16)
        wukt = jnp.transpose(w_uk[l], (2, 1, 0)).reshape(N_HEADS * QK_NOPE, KV_LORA).astype(BF16)
        wukt_pad = jnp.pad(jnp.transpose(w_uk[l], (1, 2, 0)),
                           ((0, 0), (0, HEAD_PAD - QK_NOPE), (0, 0))).astype(BF16)
        gpad = jnp.zeros((HEAD_PAD - QK_HEAD,), F32)
        gq = jnp.concatenate([q_head_norm[l] * (SCALE * LOG2E), gpad])[None, :]
        gk = jnp.concatenate([k_head_norm[l], gpad])[None, :]
        score_bound = SHIFT_SLACK * QK_HEAD * jnp.max(jnp.abs(gq)) * jnp.max(jnp.abs(gk))
        lane_id = jnp.arange(HEAD_PAD)
        aug = jnp.zeros((SUBLANES, HEAD_PAD), F32)
        aug = aug.at[0].set(jnp.where(lane_id == QK_HEAD, -score_bound, 0.0))
        aug = aug.at[1].set(jnp.where(lane_id == QK_HEAD, 1.0, 0.0))
        aug = aug.at[2].set(jnp.where(lane_id == V_HEAD, 1.0, 0.0))
        row = lambda v: v[None, :]
        ffn1 = (row(norm_ffn1[l]), ffn1_w_gu[l].astype(BF16), ffn1_w_down[l].astype(BF16))
        ffn2 = (row(norm_ffn2[l]), ffn2_w_gu[l].astype(BF16), ffn2_w_down[l].astype(BF16))
        proj_w = (row(norm_mix[l]), w1, row(q_a_norm[l]), row(kv_a_norm[l]), wq)
        conv_p = (conv_w[l], row(conv_b[l]), row(conv_ln_g[l]), row(conv_ln_b[l]))
        merge_w = (w_gates, w_conv_out[l].astype(BF16), w_attn_out[l].astype(BF16), w_out[l].astype(BF16))

        x1 = _ffn(yp, *ffn1)
        u, q, c, kpe, k, v = _inproj(x1, *proj_w, cos_p, sin_p, gq, gk, wk, wv_pad, aug)
        u3 = u.reshape(nb, t, CONV_DIM)
        a = _conv_prompt(u3, *conv_p).reshape(mp, CONV_DIM)
        o = lax.cond(score_bound < MAX_SCORE_SHIFT,
                     functools.partial(_prompt_attn, bounded=True),
                     functools.partial(_prompt_attn, bounded=False),
                     q.reshape(nb, t, _QW), k.reshape(nb, t, _QW), v.reshape(nb, t, _QW))
        o = o.reshape(mp, N_HEADS * V_HEAD)
        x2 = _merge(x1, row(norm_mix[l]), a, o, *merge_w)
        yp = _ffn(x2, *ffn2)
        outs[0].append(c.reshape(nb, t, KV_LORA))
        outs[1].append(kpe[:, QK_NOPE:QK_HEAD].reshape(nb, t, QK_ROPE))
        outs[2].append(u3[:, t - (CONV_WIDTH - 1):, :])

        x1 = _ffn(ys, *ffn1)
        u, q, c, kpe = _inproj(x1, *proj_w, cos_s, sin_s, gq, gk)
        ext = jnp.concatenate([state_conv[l], u.reshape(ns, ts, CONV_DIM)], axis=1)
        a = _conv_sample(ext, *conv_p).reshape(ms, CONV_DIM)
        c3 = c.reshape(ns, ts, KV_LORA)
        kpe3 = kpe[:, QK_NOPE:QK_HEAD].reshape(ns, ts, QK_ROPE)
        qabs = _qabs(q, wukt_pad)
        qabs = qabs.reshape(N_HEADS, ns, ts, KV_LORA).transpose(1, 2, 0, 3).reshape(ns, ts * N_HEADS, KV_LORA)
        qr = q.reshape(ns, ts, N_HEADS, HEAD_PAD)[..., QK_NOPE:QK_HEAD].reshape(ns, ts * N_HEADS, QK_ROPE)
        kpe_new_t = jnp.pad(kpe3.transpose(0, 2, 1), ((0, 0), (0, 0), (0, PAGE_SIZE - ts)))
        o = _sample_attn(page_table, qabs, qr, c3, kpe_new_t, wukt, wv, cache_ckv, cache_kpe_t, l)
        x2 = _merge(x1, row(norm_mix[l]), a, o.reshape(ms, N_HEADS * V_HEAD), *merge_w)
        ys = _ffn(x2, *ffn2)
        outs[3].append(c3)
        outs[4].append(kpe3)
        outs[5].append(ext[:, ts:, :])

    return (yp.reshape(nb, t, D_MODEL), ys.reshape(ns, ts, D_MODEL),
            jnp.stack(outs[0]), jnp.stack(outs[1]), jnp.stack(outs[2]),
            jnp.stack(outs[3]), jnp.stack(outs[4]), jnp.stack(outs[5]))
```

```python
import functools

import jax
import jax.numpy as jnp
from jax import lax
from jax.experimental import pallas as pl
from jax.experimental.pallas import tpu as pltpu

D_MODEL = 1024
CONV_DIM = 512
CONV_WIDTH = 31
N_HEADS = 8
QK_NOPE = 64
QK_ROPE = 32
QK_HEAD = QK_NOPE + QK_ROPE
V_HEAD = 64
Q_LORA = 384
KV_LORA = 256
ROPE_BASE = 10000.0
SCALE = QK_HEAD ** -0.5
LOG2E = 1.4426950408889634
D_FF = 2816
EPS = 1e-6
PAGE_SIZE = 128

SUBLANES = 8
HEAD_PAD = 128
FF_CHUNK = 256
CONV_HALO = 32
SAMPLE_PAGES_PER_CHUNK = 16
SAMPLE_SLOTS = 6
NEG_BIG = -1e30
SHIFT_SLACK = 1.01
MAX_SCORE_SHIFT = 50.0
VMEM_LIMIT = 56 * 1024 * 1024

BF16 = jnp.bfloat16
F32 = jnp.float32


def _row_tile(m, want=512):
    return want if m % want == 0 else m


def _rms(x, g):
    return x * lax.rsqrt(jnp.mean(x * x, axis=-1, keepdims=True) + EPS) * g


def _dot(a, b):
    return jnp.dot(a, b, preferred_element_type=F32)


def _dot_nt(a, b):
    return lax.dot_general(a, b, (((1,), (1,)), ((), ())), preferred_element_type=F32)


def _const_spec(shape):
    nd = len(shape)
    return pl.BlockSpec(shape, lambda *_: (0,) * nd)


def _params(sem):
    return pltpu.CompilerParams(dimension_semantics=sem, vmem_limit_bytes=VMEM_LIMIT)


def _ffn_kernel(x_ref, g_ref, wgu_ref, wd_ref, o_ref, acc_ref):
    x = x_ref[...]
    h = _rms(x, g_ref[...]).astype(BF16)
    for c in range(D_FF // FF_CHUNK):
        lo = c * FF_CHUNK
        gate = _dot(h, wgu_ref[:, lo:lo + FF_CHUNK])
        up = _dot(h, wgu_ref[:, D_FF + lo:D_FF + lo + FF_CHUNK])
        act = (gate * jax.nn.sigmoid(gate) * up).astype(BF16)
        part = _dot(act, wd_ref[lo:lo + FF_CHUNK, :])
        if c == 0:
            acc_ref[...] = part
        else:
            acc_ref[...] += part
    o_ref[...] = x + 0.5 * acc_ref[...]


def _ffn(x, g, wgu, wd):
    m = x.shape[0]
    tm = _row_tile(m)
    return pl.pallas_call(
        _ffn_kernel,
        out_shape=jax.ShapeDtypeStruct((m, D_MODEL), F32),
        grid=(m // tm,),
        in_specs=[
            pl.BlockSpec((tm, D_MODEL), lambda i: (i, 0)),
            _const_spec((1, D_MODEL)),
            _const_spec((D_MODEL, 2 * D_FF)),
            _const_spec((D_FF, D_MODEL)),
        ],
        out_specs=pl.BlockSpec((tm, D_MODEL), lambda i: (i, 0)),
        scratch_shapes=[pltpu.VMEM((tm, D_MODEL), F32)],
        compiler_params=_params(("parallel",)),
        name="ffn",
    )(x, g, wgu, wd)


_W1_QA = 2 * CONV_DIM
_W1_CKV = _W1_QA + Q_LORA
_W1_KPE = _W1_CKV + KV_LORA
_W1_KPE_SW = _W1_KPE + HEAD_PAD
_W1_COLS = _W1_KPE_SW + HEAD_PAD
_QW = N_HEADS * HEAD_PAD


def _inproj_kernel(x_ref, g_ref, w1_ref, gqa_ref, gkv_ref, wq_ref, cos_ref, sin_ref,
                   gq_ref, gk_ref, *rest, with_keys):
    if with_keys:
        wk_ref, wv_ref, aug_ref, u_ref, q_ref, c_ref, kpe_ref, k_ref, v_ref = rest
    else:
        u_ref, q_ref, c_ref, kpe_ref = rest
    h = _rms(x_ref[...], g_ref[...]).astype(BF16)
    proj = _dot(h, w1_ref[...])
    u_ref[...] = proj[:, :CONV_DIM] * jax.nn.sigmoid(proj[:, CONV_DIM:2 * CONV_DIM])

    cos = cos_ref[...]
    sin = sin_ref[...]
    qa = _rms(proj[:, _W1_QA:_W1_CKV], gqa_ref[...]).astype(BF16)
    qq = _dot(qa, wq_ref[...])
    gq = gq_ref[...]
    gk = gk_ref[...]
    for hd in range(N_HEADS):
        lo = hd * HEAD_PAD
        qh = qq[:, lo:lo + HEAD_PAD] * cos + qq[:, _QW + lo:_QW + lo + HEAD_PAD] * sin
        qh = qh * lax.rsqrt(jnp.sum(qh * qh, axis=-1, keepdims=True) * (1.0 / QK_HEAD) + EPS) * gq
        if with_keys:
            qh = qh + aug_ref[0:1, :]
        else:
            qh = qh * gk
        q_ref[:, lo:lo + HEAD_PAD] = qh.astype(BF16)

    c = _rms(proj[:, _W1_CKV:_W1_KPE], gkv_ref[...])
    c_ref[...] = c
    kpe = proj[:, _W1_KPE:_W1_KPE_SW] * cos + proj[:, _W1_KPE_SW:_W1_COLS] * sin
    kpe_ref[...] = kpe

    if with_keys:
        cb = c.astype(BF16)
        kn = _dot(cb, wk_ref[...])
        for hd in range(N_HEADS):
            lo = hd * HEAD_PAD
            kh = kn[:, lo:lo + HEAD_PAD] + kpe
            kh = kh * lax.rsqrt(jnp.sum(kh * kh, axis=-1, keepdims=True) * (1.0 / QK_HEAD) + EPS) * gk
            k_ref[:, lo:lo + HEAD_PAD] = (kh + aug_ref[1:2, :]).astype(BF16)
        v_ref[...] = (_dot(cb, wv_ref[...]) + jnp.tile(aug_ref[2:3, :], (1, N_HEADS))).astype(BF16)


def _inproj(x, g, w1, gqa, gkv, wq, cos_tab, sin_tab, gq, gk, wk=None, wv=None, aug=None):
    m = x.shape[0]
    tm = _row_tile(m)
    with_keys = wk is not None
    n_tab = cos_tab.shape[0] // tm
    row = lambda i: (i, 0)
    tab = lambda i: (i % n_tab, 0)
    in_specs = [
        pl.BlockSpec((tm, D_MODEL), row),
        _const_spec((1, D_MODEL)),
        _const_spec((D_MODEL, _W1_COLS)),
        _const_spec((1, Q_LORA)),
        _const_spec((1, KV_LORA)),
        _const_spec((Q_LORA, 2 * _QW)),
        pl.BlockSpec((tm, HEAD_PAD), tab),
        pl.BlockSpec((tm, HEAD_PAD), tab),
        _const_spec((1, HEAD_PAD)),
        _const_spec((1, HEAD_PAD)),
    ]
    args = [x, g, w1, gqa, gkv, wq, cos_tab, sin_tab, gq, gk]
    out_shape = [
        jax.ShapeDtypeStruct((m, CONV_DIM), F32),
        jax.ShapeDtypeStruct((m, _QW), BF16),
        jax.ShapeDtypeStruct((m, KV_LORA), F32),
        jax.ShapeDtypeStruct((m, HEAD_PAD), F32),
    ]
    out_specs = [
        pl.BlockSpec((tm, CONV_DIM), row),
        pl.BlockSpec((tm, _QW), row),
        pl.BlockSpec((tm, KV_LORA), row),
        pl.BlockSpec((tm, HEAD_PAD), row),
    ]
    if with_keys:
        in_specs += [_const_spec((KV_LORA, _QW)), _const_spec((KV_LORA, _QW)), _const_spec((SUBLANES, HEAD_PAD))]
        args += [wk, wv, aug]
        out_shape += [jax.ShapeDtypeStruct((m, _QW), BF16), jax.ShapeDtypeStruct((m, _QW), BF16)]
        out_specs += [pl.BlockSpec((tm, _QW), row), pl.BlockSpec((tm, _QW), row)]
    return pl.pallas_call(
        functools.partial(_inproj_kernel, with_keys=with_keys),
        out_shape=out_shape,
        grid=(m // tm,),
        in_specs=in_specs,
        out_specs=out_specs,
        compiler_params=_params(("parallel",)),
        name="inproj_keys" if with_keys else "inproj",
    )(*args)


def _conv_tail(acc, b, g, beta):
    y = acc + b
    mu = jnp.mean(y, axis=-1, keepdims=True)
    d = y - mu
    var = jnp.mean(d * d, axis=-1, keepdims=True)
    z = d * lax.rsqrt(var + EPS) * g + beta
    return (z * jax.nn.sigmoid(z)).astype(BF16)


def _conv_prompt_kernel(cur_ref, prev_ref, w_ref, b_ref, g_ref, beta_ref, o_ref, ext_ref, shift_ref):
    tt = cur_ref.shape[1]
    first = pl.program_id(1) == 0
    prev = prev_ref[0]
    ext_ref[0:CONV_HALO, :] = jnp.where(first, 0.0, prev)
    ext_ref[CONV_HALO:, :] = cur_ref[0]
    span = shift_ref.shape[1]
    for s in range(1, SUBLANES):
        shift_ref[s - 1] = ext_ref[s:s + span, :]
    base = CONV_HALO - (CONV_WIDTH - 1)
    acc = None
    for j in range(CONV_WIDTH):
        s = (base + j) % SUBLANES
        lo = base + j - s
        rows = ext_ref[lo:lo + tt, :] if s == 0 else shift_ref[s - 1, lo:lo + tt, :]
        term = w_ref[j:j + 1, :] * rows
        acc = term if acc is None else acc + term
    o_ref[0] = _conv_tail(acc, b_ref[...], g_ref[...], beta_ref[...])


def _conv_prompt(u, w, b, g, beta):
    nb, t, _ = u.shape
    tt = _row_tile(t, 256)
    per = tt // CONV_HALO
    return pl.pallas_call(
        _conv_prompt_kernel,
        out_shape=jax.ShapeDtypeStruct((nb, t, CONV_DIM), BF16),
        grid=(nb, t // tt),
        in_specs=[
            pl.BlockSpec((1, tt, CONV_DIM), lambda bi, i: (bi, i, 0)),
            pl.BlockSpec((1, CONV_HALO, CONV_DIM), lambda bi, i: (bi, jnp.maximum(i * per - 1, 0), 0)),
            _const_spec((CONV_WIDTH, CONV_DIM)),
            _const_spec((1, CONV_DIM)),
            _const_spec((1, CONV_DIM)),
            _const_spec((1, CONV_DIM)),
        ],
        out_specs=pl.BlockSpec((1, tt, CONV_DIM), lambda bi, i: (bi, i, 0)),
        scratch_shapes=[pltpu.VMEM((tt + CONV_HALO, CONV_DIM), F32),
                        pltpu.VMEM((SUBLANES - 1, tt + CONV_HALO - SUBLANES, CONV_DIM), F32)],
        compiler_params=_params(("parallel", "parallel")),
        name="conv_prompt",
    )(u, u, w, b, g, beta)


def _conv_sample_kernel(ext_ref, w_ref, b_ref, g_ref, beta_ref, o_ref):
    t = o_ref.shape[1]
    acc = w_ref[0:1, :] * ext_ref[:, 0:t, :]
    for j in range(1, CONV_WIDTH):
        acc = acc + w_ref[j:j + 1, :] * ext_ref[:, j:j + t, :]
    o_ref[...] = _conv_tail(acc, b_ref[...], g_ref[...], beta_ref[...])


def _conv_sample(ext, w, b, g, beta):
    nb, rows, _ = ext.shape
    t = rows - (CONV_WIDTH - 1)
    bs = 16 if nb % 16 == 0 else nb
    return pl.pallas_call(
        _conv_sample_kernel,
        out_shape=jax.ShapeDtypeStruct((nb, t, CONV_DIM), BF16),
        grid=(nb // bs,),
        in_specs=[
            pl.BlockSpec((bs, rows, CONV_DIM), lambda i: (i, 0, 0)),
            _const_spec((CONV_WIDTH, CONV_DIM)),
            _const_spec((1, CONV_DIM)),
            _const_spec((1, CONV_DIM)),
            _const_spec((1, CONV_DIM)),
        ],
        out_specs=pl.BlockSpec((bs, t, CONV_DIM), lambda i: (i, 0, 0)),
        compiler_params=_params(("parallel",)),
        name="conv_sample",
    )(ext, w, b, g, beta)


def _prompt_attn_kernel(q_ref, k_ref, v_ref, o_ref, *scratch, tk, bounded):
    if bounded:
        (acc_ref,) = scratch
    else:
        m_ref, acc_ref = scratch
    tq = q_ref.shape[1]
    qi = pl.program_id(1)
    row = lax.broadcasted_iota(jnp.int32, (tq, tk), 0)
    col = lax.broadcasted_iota(jnp.int32, (tq, tk), 1)
    lane = lax.broadcasted_iota(jnp.int32, (tq, HEAD_PAD), 1)

    def step(hd, j, masked):
        lo = hd * HEAD_PAD
        start = pl.multiple_of(j * tk, tk)
        kh = k_ref[0, pl.ds(start, tk), lo:lo + HEAD_PAD]
        vh = v_ref[0, pl.ds(start, tk), lo:lo + HEAD_PAD]
        s = _dot_nt(q_ref[0, :, lo:lo + HEAD_PAD], kh)
        if masked:
            s = jnp.where(col <= row, s, NEG_BIG)
        if bounded:
            acc_ref[hd] += _dot(jnp.exp2(s).astype(BF16), vh)
        else:
            m_old = m_ref[hd]
            m_new = jnp.maximum(m_old, jnp.max(s, axis=-1, keepdims=True))
            acc_ref[hd] = jnp.exp2(m_old - m_new) * acc_ref[hd] + _dot(jnp.exp2(s - m_new).astype(BF16), vh)
            m_ref[hd] = m_new

    if not bounded:
        m_ref[...] = jnp.full(m_ref.shape, NEG_BIG, F32)
    acc_ref[...] = jnp.zeros(acc_ref.shape, F32)

    def body(j, carry):
        for hd in range(N_HEADS):
            step(hd, j, False)
        return carry

    lax.fori_loop(0, qi, body, 0)
    for hd in range(N_HEADS):
        step(hd, qi, True)
    for pair in range(N_HEADS // 2):
        even = acc_ref[2 * pair]
        odd = acc_ref[2 * pair + 1]
        even = even / even[:, V_HEAD:V_HEAD + 1]
        odd = pltpu.roll(odd / odd[:, V_HEAD:V_HEAD + 1], V_HEAD, 1)
        vlo = pair * HEAD_PAD
        o_ref[0, :, vlo:vlo + HEAD_PAD] = jnp.where(lane < V_HEAD, even, odd).astype(BF16)


def _prompt_attn(q, k, v, bounded):
    nb, t, _ = q.shape
    tq = _row_tile(t)
    scratch = [pltpu.VMEM((N_HEADS, tq, HEAD_PAD), F32)]
    if not bounded:
        scratch = [pltpu.VMEM((N_HEADS, tq, 1), F32)] + scratch
    return pl.pallas_call(
        functools.partial(_prompt_attn_kernel, tk=tq, bounded=bounded),
        out_shape=jax.ShapeDtypeStruct((nb, t, N_HEADS * V_HEAD), BF16),
        grid=(nb, t // tq),
        in_specs=[
            pl.BlockSpec((1, tq, _QW), lambda bi, i: (bi, i, 0)),
            pl.BlockSpec((1, t, _QW), lambda bi, i: (bi, 0, 0)),
            pl.BlockSpec((1, t, _QW), lambda bi, i: (bi, 0, 0)),
        ],
        out_specs=pl.BlockSpec((1, tq, N_HEADS * V_HEAD), lambda bi, i: (bi, i, 0)),
        scratch_shapes=scratch,
        compiler_params=_params(("parallel", "arbitrary")),
        name="prompt_attn_bounded" if bounded else "prompt_attn",
    )(q, k, v)


def _qabs_kernel(q_ref, w_ref, o_ref):
    o_ref[0] = _dot(q_ref[...], w_ref[0]).astype(BF16)


def _qabs(q, wukt):
    m = q.shape[0]
    return pl.pallas_call(
        _qabs_kernel,
        out_shape=jax.ShapeDtypeStruct((N_HEADS, m, KV_LORA), BF16),
        grid=(N_HEADS,),
        in_specs=[
            pl.BlockSpec((m, HEAD_PAD), lambda h: (0, h)),
            pl.BlockSpec((1, HEAD_PAD, KV_LORA), lambda h: (h, 0, 0)),
        ],
        out_specs=pl.BlockSpec((1, m, KV_LORA), lambda h: (h, 0, 0)),
        compiler_params=_params(("parallel",)),
        name="qabs",
    )(q, wukt)


def _sample_attn_kernel(pt_ref, qabs_ref, qr_ref, cnew_ref, kpenew_ref, wukt_ref, wuv_ref,
                        ckv_hbm, kpe_hbm, o_ref,
                        cbuf, kbuf, sem, m_ref, l_ref, acc_ref, wq_all, s_ref, cb_ref,
                        *, layer, pages_per_chunk, n_seq):
    b = pl.program_id(0)
    n_pages = pt_ref.shape[1]
    n_chunks = n_pages // pages_per_chunk
    chunk = pages_per_chunk * PAGE_SIZE
    t_new = cnew_ref.shape[1]
    rows = N_HEADS * t_new

    n_slots = cbuf.shape[0]
    lookahead = n_slots - 2
    total = n_seq * n_chunks

    def copies(g):
        src = jnp.minimum(g, total - 1)
        seq = src // n_chunks
        ck = src % n_chunks
        slot = g % n_slots
        out = []
        for p in range(pages_per_chunk):
            page = pt_ref[seq, ck * pages_per_chunk + p]
            dst = pl.ds(p * PAGE_SIZE, PAGE_SIZE)
            out.append(pltpu.make_async_copy(ckv_hbm.at[layer, page], cbuf.at[slot, dst], sem.at[0, slot]))
            out.append(pltpu.make_async_copy(kpe_hbm.at[layer, page], kbuf.at[slot, :, dst], sem.at[1, slot]))
        return out

    @pl.when(b == 0)
    def _():
        for g0 in range(lookahead):
            for cp in copies(g0):
                cp.start()

    m_ref[...] = jnp.full(m_ref.shape, NEG_BIG, F32)
    l_ref[...] = jnp.zeros(l_ref.shape, F32)
    acc_ref[...] = jnp.zeros(acc_ref.shape, F32)

    n_k = N_HEADS * QK_NOPE
    wq_all[0:n_k, :] = wukt_ref[...]
    wq_all[n_k:, :] = qabs_ref[0]
    qr = qr_ref[0]

    def scores(c32, kpt32):
        n = c32.shape[0]
        cb = c32.astype(BF16)
        kt = _dot_nt(wq_all[...], cb)
        ss = jnp.sum((kt[:n_k] * kt[:n_k]).reshape(QK_NOPE, N_HEADS, n), axis=0)
        ss = ss + jnp.sum(kpt32 * kpt32, axis=0, keepdims=True)
        r = lax.rsqrt(ss * (1.0 / QK_HEAD) + EPS)
        s = kt[n_k:] + _dot(qr, kpt32.astype(BF16))
        return (s.reshape(t_new, N_HEADS, n) * r[None, :, :]).reshape(rows, n), cb

    def update(s, cb):
        m_old = m_ref[...]
        m_new = jnp.maximum(m_old, jnp.max(s, axis=-1, keepdims=True))
        alpha = jnp.exp2(m_old - m_new)
        p = jnp.exp2(s - m_new)
        l_ref[...] = alpha * l_ref[...] + jnp.sum(p, axis=-1, keepdims=True)
        acc_ref[...] = alpha * acc_ref[...] + _dot(p.astype(BF16), cb)
        m_ref[...] = m_new

    def half_step(ck, par, fold_previous=True, with_new_rows=False, waits=0):
        g = b * n_chunks + ck
        slot = g % n_slots

        width = chunk + PAGE_SIZE if with_new_rows else chunk
        if with_new_rows:
            cbuf[slot, chunk:, :] = jnp.zeros((PAGE_SIZE, KV_LORA), F32)
            cbuf[slot, chunk:chunk + t_new, :] = cnew_ref[0]
            kbuf[slot, :, chunk:] = kpenew_ref[0]
        for i in range(waits):
            for cp in copies(g + i):
                cp.wait()
        s_new, cb_new = scores(cbuf[slot, 0:width, :], kbuf[slot, :, 0:width])
        if with_new_rows:
            qpos = lax.broadcasted_iota(jnp.int32, (rows, width), 0) // N_HEADS
            kpos = lax.broadcasted_iota(jnp.int32, (rows, width), 1) - chunk
            s_new = jnp.where(kpos <= qpos, s_new, NEG_BIG)
        for cp in copies(g + lookahead):
            cp.start()
        if fold_previous:
            update(s_ref[1 - par, :, 0:chunk], cb_ref[1 - par, 0:chunk, :])
        s_ref[par, :, 0:width] = s_new
        cb_ref[par, 0:width, :] = cb_new

    def pair_body(pr, carry):
        half_step(2 * pr, 0, waits=2)
        half_step(2 * pr + 1, 1)
        return carry

    if n_chunks == 2:
        half_step(0, 0, fold_previous=False, waits=2)
    else:
        half_step(0, 0, fold_previous=False, waits=2)
        half_step(1, 1)
        lax.fori_loop(1, n_chunks // 2 - 1, pair_body, 0)
        half_step(n_chunks - 2, 0, waits=2)
    half_step(n_chunks - 1, 1, with_new_rows=True)
    update(s_ref[1], cb_ref[1])

    o_ref[0] = (acc_ref[...] / l_ref[...]).astype(BF16)

    @pl.when(b == n_seq - 1)
    def _():
        for k in range(lookahead):
            for cp in copies(total + k):
                cp.wait()


def _sample_attn(page_table, qabs, qr, c_new, kpe_new, wukt, wuv, cache_ckv, cache_kpe, layer):
    nb, n_pages = page_table.shape
    t_new = c_new.shape[1]
    rows = N_HEADS * t_new
    assert n_pages % 2 == 0, "the sample kernel pipelines cache chunks in pairs"
    ppc = SAMPLE_PAGES_PER_CHUNK if n_pages % (2 * SAMPLE_PAGES_PER_CHUNK) == 0 else n_pages // 2
    chunk = ppc * PAGE_SIZE
    grid_spec = pltpu.PrefetchScalarGridSpec(
        num_scalar_prefetch=1,
        grid=(nb,),
        in_specs=[
            pl.BlockSpec((1, rows, KV_LORA), lambda b, pt: (b, 0, 0)),
            pl.BlockSpec((1, rows, QK_ROPE), lambda b, pt: (b, 0, 0)),
            pl.BlockSpec((1, t_new, KV_LORA), lambda b, pt: (b, 0, 0)),
            pl.BlockSpec((1, QK_ROPE, PAGE_SIZE), lambda b, pt: (b, 0, 0)),
            pl.BlockSpec((N_HEADS * QK_NOPE, KV_LORA), lambda b, pt: (0, 0)),
            pl.BlockSpec((KV_LORA, N_HEADS * V_HEAD), lambda b, pt: (0, 0)),
            pl.BlockSpec(memory_space=pl.ANY),
            pl.BlockSpec(memory_space=pl.ANY),
        ],
        out_specs=pl.BlockSpec((1, rows, KV_LORA), lambda b, pt: (b, 0, 0)),
        scratch_shapes=[
            pltpu.VMEM((SAMPLE_SLOTS, chunk + PAGE_SIZE, KV_LORA), F32),
            pltpu.VMEM((SAMPLE_SLOTS, QK_ROPE, chunk + PAGE_SIZE), F32),
            pltpu.SemaphoreType.DMA((2, SAMPLE_SLOTS)),
            pltpu.VMEM((rows, 1), F32),
            pltpu.VMEM((rows, 1), F32),
            pltpu.VMEM((rows, KV_LORA), F32),
            pltpu.VMEM((N_HEADS * QK_NOPE + rows, KV_LORA), BF16),
            pltpu.VMEM((2, rows, chunk + PAGE_SIZE), F32),
            pltpu.VMEM((2, chunk + PAGE_SIZE, KV_LORA), BF16),
        ],
    )
    return pl.pallas_call(
        functools.partial(_sample_attn_kernel, layer=layer, pages_per_chunk=ppc, n_seq=nb),
        out_shape=jax.ShapeDtypeStruct((nb, rows, KV_LORA), BF16),
        grid_spec=grid_spec,
        compiler_params=_params(("arbitrary",)),
        name="sample_attn",
    )(page_table, qabs, qr, c_new, kpe_new, wukt, wuv, cache_ckv, cache_kpe)


def _uv_kernel(x_ref, w_ref, o_ref):
    tn = o_ref.shape[0]
    full = _dot(x_ref[...], w_ref[...])
    full = full.reshape(tn, N_HEADS, N_HEADS * V_HEAD)
    own = (lax.broadcasted_iota(jnp.int32, (N_HEADS, N_HEADS * V_HEAD), 1) // V_HEAD
           == lax.broadcasted_iota(jnp.int32, (N_HEADS, N_HEADS * V_HEAD), 0))
    o_ref[...] = jnp.sum(jnp.where(own[None], full, 0.0), axis=1).astype(BF16)


def _uv(o_lat, wuv):
    m = o_lat.shape[0] // N_HEADS
    tn = 128 if m % 128 == 0 else m
    return pl.pallas_call(
        _uv_kernel,
        out_shape=jax.ShapeDtypeStruct((m, N_HEADS * V_HEAD), BF16),
        grid=(m // tn,),
        in_specs=[pl.BlockSpec((tn * N_HEADS, KV_LORA), lambda i: (i, 0)),
                  _const_spec((KV_LORA, N_HEADS * V_HEAD))],
        out_specs=pl.BlockSpec((tn, N_HEADS * V_HEAD), lambda i: (i, 0)),
        compiler_params=_params(("parallel",)),
        name="uv",
    )(o_lat, wuv)


def _merge_kernel(x_ref, g_ref, a_ref, o_ref, wg_ref, wc_ref, wa_ref, wo_ref, y_ref):
    x = x_ref[...]
    h = _rms(x, g_ref[...]).astype(BF16)
    gates = jax.nn.sigmoid(_dot(h, wg_ref[...]))
    y_conv = _dot(a_ref[...], wc_ref[...])
    y_attn = _dot(o_ref[...], wa_ref[...])
    mix = gates[:, :D_MODEL] * y_conv + gates[:, D_MODEL:] * y_attn
    y_ref[...] = x + _dot(mix.astype(BF16), wo_ref[...])


def _merge(x, g, a, o, wg, wc, wa, wo):
    m = x.shape[0]
    tm = _row_tile(m)
    row = lambda i: (i, 0)
    return pl.pallas_call(
        _merge_kernel,
        out_shape=jax.ShapeDtypeStruct((m, D_MODEL), F32),
        grid=(m // tm,),
        in_specs=[
            pl.BlockSpec((tm, D_MODEL), row),
            _const_spec((1, D_MODEL)),
            pl.BlockSpec((tm, CONV_DIM), row),
            pl.BlockSpec((tm, N_HEADS * V_HEAD), row),
            _const_spec((D_MODEL, 2 * D_MODEL)),
            _const_spec((CONV_DIM, D_MODEL)),
            _const_spec((N_HEADS * V_HEAD, D_MODEL)),
            _const_spec((D_MODEL, D_MODEL)),
        ],
        out_specs=pl.BlockSpec((tm, D_MODEL), row),
        compiler_params=_params(("parallel",)),
        name="merge",
    )(x, g, a, o, wg, wc, wa, wo)


def _rotate_half_cols(w):
    half = QK_ROPE // 2
    return jnp.concatenate([-w[..., half:], w[..., :half]], axis=-1)


def _rope_tables(pos):
    inv = ROPE_BASE ** (-jnp.arange(0, QK_ROPE, 2, dtype=F32) / QK_ROPE)
    ang = pos.astype(F32)[:, None] * inv[None, :]
    cos, sin = jnp.cos(ang), jnp.sin(ang)
    n = pos.shape[0]
    pad = jnp.zeros((n, HEAD_PAD - QK_HEAD), F32)
    cos_tab = jnp.concatenate([jnp.ones((n, QK_NOPE), F32), cos, cos, pad], axis=-1)
    sin_tab = jnp.concatenate([jnp.zeros((n, QK_NOPE), F32), sin, sin, pad], axis=-1)
    return cos_tab, sin_tab


def kernel(x_prompt, x_sample, cache_ckv, cache_kpe, state_conv, page_table, norm_ffn1, ffn1_w_gu, ffn1_w_down, norm_mix, w_in, conv_w, conv_b, conv_ln_g, conv_ln_b, w_conv_out, q_a_norm, w_uq, kv_a_norm, w_uk, w_uv, q_head_norm, k_head_norm, w_attn_out, w_out, norm_ffn2, ffn2_w_gu, ffn2_w_down):
    depth = norm_ffn1.shape[0]
    nb, t, _ = x_prompt.shape
    ns, ts, _ = x_sample.shape
    past = page_table.shape[1] * PAGE_SIZE
    mp, ms = nb * t, ns * ts

    cos_p, sin_p = _rope_tables(jnp.arange(t))
    cos_s, sin_s = _rope_tables(past + jnp.arange(ts))
    tms = _row_tile(ms)
    cos_s = jnp.tile(cos_s, (tms // ts, 1))
    sin_s = jnp.tile(sin_s, (tms // ts, 1))

    cache_kpe_t = jnp.swapaxes(cache_kpe, 2, 3)
    yp = x_prompt.reshape(mp, D_MODEL)
    ys = x_sample.reshape(ms, D_MODEL)
    outs = [[] for _ in range(6)]
    for l in range(depth):
        wi = w_in[l]
        w_kpe = wi[:, _W1_KPE:_W1_KPE + QK_ROPE]
        zeros_nope = jnp.zeros((D_MODEL, QK_NOPE), F32)
        zeros_pad = jnp.zeros((D_MODEL, HEAD_PAD - QK_HEAD), F32)
        w1 = jnp.concatenate([wi[:, :_W1_KPE], zeros_nope, w_kpe, zeros_pad,
                              zeros_nope, _rotate_half_cols(w_kpe), zeros_pad], axis=-1).astype(BF16)
        w_gates = wi[:, _W1_KPE + QK_ROPE:].astype(BF16)
        wq3 = w_uq[l].reshape(Q_LORA, N_HEADS, QK_HEAD)
        q_nope, q_rope = wq3[..., :QK_NOPE], wq3[..., QK_NOPE:]
        zq_pad = jnp.zeros((Q_LORA, N_HEADS, HEAD_PAD - QK_HEAD), F32)
        zq_nope = jnp.zeros_like(q_nope)
        wq = jnp.concatenate([
            jnp.concatenate([q_nope, q_rope, zq_pad], axis=-1).reshape(Q_LORA, _QW),
            jnp.concatenate([zq_nope, _rotate_half_cols(q_rope), zq_pad], axis=-1).reshape(Q_LORA, _QW),
        ], axis=-1).astype(BF16)
        wk = jnp.pad(w_uk[l], ((0, 0), (0, 0), (0, HEAD_PAD - QK_NOPE))).reshape(KV_LORA, _QW).astype(BF16)
        wv = w_uv[l].reshape(KV_LORA, N_HEADS * V_HEAD).astype(BF16)
        wv_pad = jnp.pad(w_uv[l], ((0, 0), (0, 0), (0, HEAD_PAD - V_HEAD))).reshape(KV_LORA, _QW).astype(BF16)
        wukt = jnp.transpose(w_uk[l], (2, 1, 0)).reshape(N_HEADS * QK_NOPE, KV_LORA).astype(BF16)
        wukt_pad = jnp.pad(jnp.transpose(w_uk[l], (1, 2, 0)),
                           ((0, 0), (0, HEAD_PAD - QK_NOPE), (0, 0))).astype(BF16)
        gpad = jnp.zeros((HEAD_PAD - QK_HEAD,), F32)
        gq = jnp.concatenate([q_head_norm[l] * (SCALE * LOG2E), gpad])[None, :]
        gk = jnp.concatenate([k_head_norm[l], gpad])[None, :]
        score_bound = SHIFT_SLACK * QK_HEAD * jnp.max(jnp.abs(gq)) * jnp.max(jnp.abs(gk))
        lane_id = jnp.arange(HEAD_PAD)
        aug = jnp.zeros((SUBLANES, HEAD_PAD), F32)
        aug = aug.at[0].set(jnp.where(lane_id == QK_HEAD, -score_bound, 0.0))
        aug = aug.at[1].set(jnp.where(lane_id == QK_HEAD, 1.0, 0.0))
        aug = aug.at[2].set(jnp.where(lane_id == V_HEAD, 1.0, 0.0))
        row = lambda v: v[None, :]
        ffn1 = (row(norm_ffn1[l]), ffn1_w_gu[l].astype(BF16), ffn1_w_down[l].astype(BF16))
        ffn2 = (row(norm_ffn2[l]), ffn2_w_gu[l].astype(BF16), ffn2_w_down[l].astype(BF16))
        proj_w = (row(norm_mix[l]), w1, row(q_a_norm[l]), row(kv_a_norm[l]), wq)
        conv_p = (conv_w[l], row(conv_b[l]), row(conv_ln_g[l]), row(conv_ln_b[l]))
        merge_w = (w_gates, w_conv_out[l].astype(BF16), w_attn_out[l].astype(BF16), w_out[l].astype(BF16))

        x1 = _ffn(yp, *ffn1)
        u, q, c, kpe, k, v = _inproj(x1, *proj_w, cos_p, sin_p, gq, gk, wk, wv_pad, aug)
        u3 = u.reshape(nb, t, CONV_DIM)
        a = _conv_prompt(u3, *conv_p).reshape(mp, CONV_DIM)
        o = lax.cond(score_bound < MAX_SCORE_SHIFT,
                     functools.partial(_prompt_attn, bounded=True),
                     functools.partial(_prompt_attn, bounded=False),
                     q.reshape(nb, t, _QW), k.reshape(nb, t, _QW), v.reshape(nb, t, _QW))
        o = o.reshape(mp, N_HEADS * V_HEAD)
        x2 = _merge(x1, row(norm_mix[l]), a, o, *merge_w)
        yp = _ffn(x2, *ffn2)
        outs[0].append(c.reshape(nb, t, KV_LORA))
        outs[1].append(kpe[:, QK_NOPE:QK_HEAD].reshape(nb, t, QK_ROPE))
        outs[2].append(u3[:, t - (CONV_WIDTH - 1):, :])

        x1 = _ffn(ys, *ffn1)
        u, q, c, kpe = _inproj(x1, *proj_w, cos_s, sin_s, gq, gk)
        ext = jnp.concatenate([state_conv[l], u.reshape(ns, ts, CONV_DIM)], axis=1)
        a = _conv_sample(ext, *conv_p).reshape(ms, CONV_DIM)
        c3 = c.reshape(ns, ts, KV_LORA)
        kpe3 = kpe[:, QK_NOPE:QK_HEAD].reshape(ns, ts, QK_ROPE)
        qabs = _qabs(q, wukt_pad)
        qabs = qabs.reshape(N_HEADS, ns, ts, KV_LORA).transpose(1, 2, 0, 3).reshape(ns, ts * N_HEADS, KV_LORA)
        qr = q.reshape(ns, ts, N_HEADS, HEAD_PAD)[..., QK_NOPE:QK_HEAD].reshape(ns, ts * N_HEADS, QK_ROPE)
        kpe_new_t = jnp.pad(kpe3.transpose(0, 2, 1), ((0, 0), (0, 0), (0, PAGE_SIZE - ts)))
        o = _sample_attn(page_table, qabs, qr, c3, kpe_new_t, wukt, wv, cache_ckv, cache_kpe_t, l)
        o = _uv(o.reshape(ms * N_HEADS, KV_LORA), wv)
        x2 = _merge(x1, row(norm_mix[l]), a, o, *merge_w)
        ys = _ffn(x2, *ffn2)
        outs[3].append(c3)
        outs[4].append(kpe3)
        outs[5].append(ext[:, ts:, :])

    return (yp.reshape(nb, t, D_MODEL), ys.reshape(ns, ts, D_MODEL),
            jnp.stack(outs[0]), jnp.stack(outs[1]), jnp.stack(outs[2]),
            jnp.stack(outs[3]), jnp.stack(outs[4]), jnp.stack(outs[5]))
```
